```python
import numpy as np
import jax
import jax.numpy as jnp
from jax import lax

D_MODEL = 1024
BATCH = 8
SEQ = 2048
DEPTH = 4

HEAD_DIM = 64
NSA_HEADS = 8
NSA_KV_HEADS = 2
NSA_GROUP = NSA_HEADS // NSA_KV_HEADS
SB_HEADS = 8
CMP_BLOCK = 32
CMP_STRIDE = 16
CMP_HIDDEN = 128
SEL_BLOCK = 64
SEL_TOPK = 16
WINDOW = 512
Q_BLOCK = 128
ROPE_THETA = 500000.0
ROT_DIM = HEAD_DIM // 4
D_FF = 2816
CONV_W = 3
LN_EPS = 1e-5
NEG = -1e30
FORCE = 1e4
DEEPNORM_ALPHA = (2.0 * DEPTH) ** 0.25
DEEPNORM_BETA = (8.0 * DEPTH) ** -0.25

NSA_Q = NSA_HEADS * HEAD_DIM
NSA_KV = NSA_KV_HEADS * HEAD_DIM
SB_W = SB_HEADS * HEAD_DIM
SPLIT_SIZES = (NSA_Q, NSA_KV, NSA_KV, NSA_KV, NSA_KV, NSA_KV, NSA_KV, 3 * NSA_HEADS, SB_W, SB_W, SB_W, 2 * D_MODEL)
SPLIT_POINTS = tuple(int(v) for v in np.cumsum(SPLIT_SIZES)[:-1])
IN_WIDTH = int(sum(SPLIT_SIZES))

kernel_name = "nsa_stickbreaking_gated_merge_deepnorm"


def layer_norm(x, g, b):
    xf = x.astype(jnp.float32)
    mu = jnp.mean(xf, axis=-1, keepdims=True)
    var = jnp.mean(jnp.square(xf - mu), axis=-1, keepdims=True)
    y = (xf - mu) * lax.rsqrt(var + LN_EPS)
    return (y * g.astype(jnp.float32) + b.astype(jnp.float32)).astype(x.dtype)


def rotary_tables(seq):
    inv_freq = ROPE_THETA ** (-np.arange(0, ROT_DIM, 2, dtype=np.float32) / ROT_DIM)
    ang = jnp.arange(seq, dtype=jnp.float32)[:, None] * jnp.asarray(inv_freq, jnp.float32)[None, :]
    return jnp.cos(ang), jnp.sin(ang)


def partial_rope(x, cos, sin):
    half = ROT_DIM // 2
    c = cos[None, :, None, :].astype(x.dtype)
    s = sin[None, :, None, :].astype(x.dtype)
    x1 = x[..., :half]
    x2 = x[..., half:ROT_DIM]
    return jnp.concatenate([x1 * c - x2 * s, x2 * c + x1 * s, x[..., ROT_DIM:]], axis=-1)


def masked_softmax(s, mask):
    s = jnp.where(mask, s.astype(jnp.float32), NEG)
    m = jnp.max(s, axis=-1, keepdims=True)
    e = jnp.where(mask, jnp.exp(s - m), 0.0)
    return e / jnp.maximum(jnp.sum(e, axis=-1, keepdims=True), 1e-30)


def compress_blocks(k, pos, w1, b1, w2):
    b, g, s, d = k.shape
    n_cmp = (s - CMP_BLOCK) // CMP_STRIDE + 1
    idx = CMP_STRIDE * np.arange(n_cmp)[:, None] + np.arange(CMP_BLOCK)[None, :]
    blocks = k[:, :, idx] + pos.astype(k.dtype)
    hid = jax.nn.gelu(blocks.reshape(b, g, n_cmp, CMP_BLOCK * d) @ w1 + b1)
    return hid @ w2


def selection_overlap(seq):
    n_cmp = (seq - CMP_BLOCK) // CMP_STRIDE + 1
    n_sel = seq // SEL_BLOCK
    cs = np.arange(n_cmp) * CMP_STRIDE
    ce = cs + CMP_BLOCK
    ss = np.arange(n_sel) * SEL_BLOCK
    se = ss + SEL_BLOCK
    ov = np.clip(np.minimum(ce[:, None], se[None, :]) - np.maximum(cs[:, None], ss[None, :]), 0, None)
    return jnp.asarray(ov / CMP_BLOCK, dtype=jnp.float32)


def gather_blocks(blocks, idx):
    g = jax.vmap(jax.vmap(lambda bl, ix: bl[ix]))(blocks, idx)
    b, gg, tq, n, sb, d = g.shape
    return g.reshape(b, gg, tq, n * sb, d)


def nsa_attention(q, k_cmp, v_cmp, k_sel, v_sel, k_win, v_win, gates,
                  pos_k, w1_k, b1_k, w2_k, pos_v, w1_v, b1_v, w2_v):
    b, _, s, d = q.shape
    scale = d ** -0.5
    qg = q.reshape(b, NSA_KV_HEADS, NSA_GROUP, s, d)
    t = jnp.arange(s)

    kc = compress_blocks(k_cmp, pos_k, w1_k, b1_k, w2_k)
    vc = compress_blocks(v_cmp, pos_v, w1_v, b1_v, w2_v)
    n_cmp = kc.shape[2]
    cmp_end = CMP_STRIDE * jnp.arange(n_cmp) + CMP_BLOCK - 1
    m_cmp = cmp_end[None, :] <= t[:, None]
    p_cmp = masked_softmax(jnp.einsum('bghsd,bgcd->bghsc', qg, kc) * scale, m_cmp)
    o_cmp = jnp.einsum('bghsc,bgcd->bghsd', p_cmp.astype(vc.dtype), vc)

    n_sel = s // SEL_BLOCK
    n_top = min(SEL_TOPK, n_sel)
    score = jnp.einsum('bghsc,cj->bgsj', p_cmp, selection_overlap(s))
    j = jnp.arange(n_sel)[None, :]
    cur = t[:, None] // SEL_BLOCK
    forced = (j == 0) | (j == cur) | (j == cur - 1)
    valid = j * SEL_BLOCK <= t[:, None]
    score = jnp.where(forced, FORCE, jnp.where(valid, score, -FORCE))
    _, sel_idx = lax.top_k(score, n_top)

    ks_blocks = k_sel.reshape(b, NSA_KV_HEADS, n_sel, SEL_BLOCK, d)
    vs_blocks = v_sel.reshape(b, NSA_KV_HEADS, n_sel, SEL_BLOCK, d)
    kw_pad = jnp.pad(k_win, ((0, 0), (0, 0), (WINDOW, 0), (0, 0)))
    vw_pad = jnp.pad(v_win, ((0, 0), (0, 0), (WINDOW, 0), (0, 0)))

    nb = s // Q_BLOCK
    q_blk = qg.reshape(b, NSA_KV_HEADS, NSA_GROUP, nb, Q_BLOCK, d).transpose(3, 0, 1, 2, 4, 5)
    idx_blk = sel_idx.reshape(b, NSA_KV_HEADS, nb, Q_BLOCK, n_top).transpose(2, 0, 1, 3, 4)
    t0s = jnp.arange(nb) * Q_BLOCK

    def query_block(args):
        qb, ib, t0 = args
        tq = t0 + jnp.arange(Q_BLOCK)
        kg = gather_blocks(ks_blocks, ib)
        vg = gather_blocks(vs_blocks, ib)
        kpos = (ib[..., None] * SEL_BLOCK + jnp.arange(SEL_BLOCK)).reshape(b, NSA_KV_HEADS, Q_BLOCK, n_top * SEL_BLOCK)
        m_sel = (kpos <= tq[:, None])[:, :, None]
        p = masked_softmax(jnp.einsum('bghqd,bgqkd->bghqk', qb, kg) * scale, m_sel)
        o_sel = jnp.einsum('bghqk,bgqkd->bghqd', p.astype(vg.dtype), vg)
        kwb = lax.dynamic_slice_in_dim(kw_pad, t0, WINDOW + Q_BLOCK, axis=2)
        vwb = lax.dynamic_slice_in_dim(vw_pad, t0, WINDOW + Q_BLOCK, axis=2)
        kp = t0 - WINDOW + jnp.arange(WINDOW + Q_BLOCK)
        diff = tq[:, None] - kp[None, :]
        m_win = (kp[None, :] >= 0) & (diff >= 0) & (diff < WINDOW)
        p = masked_softmax(jnp.einsum('bghqd,bgkd->bghqk', qb, kwb) * scale, m_win)
        o_win = jnp.einsum('bghqk,bgkd->bghqd', p.astype(vwb.dtype), vwb)
        return o_sel, o_win

    o_sel, o_win = lax.map(query_block, (q_blk, idx_blk, t0s))

    def unblock(o):
        return o.transpose(1, 2, 3, 0, 4, 5).reshape(b, NSA_KV_HEADS, NSA_GROUP, s, d)

    g = gates.reshape(b, NSA_KV_HEADS, NSA_GROUP, s, 3)
    o = g[..., 0:1] * o_cmp + g[..., 1:2] * unblock(o_sel) + g[..., 2:3] * unblock(o_win)
    return o.reshape(b, NSA_HEADS, s, d)


def stick_breaking_attention(q, k, v):
    s, d = q.shape[2], q.shape[3]
    scale = d ** -0.5
    outs = []
    for blk in range(s // Q_BLOCK):
        t0, t1 = blk * Q_BLOCK, (blk + 1) * Q_BLOCK
        z = jnp.einsum('bhqd,bhkd->bhqk', q[:, :, t0:t1], k[:, :, :t1]).astype(jnp.float32) * scale
        tq = t0 + jnp.arange(Q_BLOCK)
        mask = jnp.arange(t1)[None, :] < tq[:, None]
        log_fail = jnp.where(mask, jax.nn.log_sigmoid(-z), 0.0)
        suffix = lax.cumsum(log_fail, axis=3, reverse=True) - log_fail
        a = jnp.where(mask, jnp.exp(jax.nn.log_sigmoid(z) + suffix), 0.0)
        outs.append(jnp.einsum('bhqk,bhkd->bhqd', a.astype(v.dtype), v[:, :, :t1]))
    return jnp.concatenate(outs, axis=2)


def token_mixer(h, cos, sin, w_in, pos_k, w1_k, b1_k, w2_k, pos_v, w1_v, b1_v, w2_v,
                w_branch_a, w_branch_b, w_out):
    b, s, _ = h.shape
    (q_a, kc, vc, ks, vs, kw, vw, g_a, q_b, k_b, v_b, g_m) = jnp.split(h @ w_in, SPLIT_POINTS, axis=-1)

    def heads(t, n, rotary):
        t = t.reshape(b, s, n, HEAD_DIM)
        if rotary:
            t = partial_rope(t, cos, sin)
        return t.transpose(0, 2, 1, 3)

    nsa_gates = jax.nn.sigmoid(g_a).reshape(b, s, NSA_HEADS, 3).transpose(0, 2, 1, 3)
    o_a = nsa_attention(heads(q_a, NSA_HEADS, True),
                        heads(kc, NSA_KV_HEADS, True), heads(vc, NSA_KV_HEADS, False),
                        heads(ks, NSA_KV_HEADS, True), heads(vs, NSA_KV_HEADS, False),
                        heads(kw, NSA_KV_HEADS, True), heads(vw, NSA_KV_HEADS, False),
                        nsa_gates, pos_k, w1_k, b1_k, w2_k, pos_v, w1_v, b1_v, w2_v)
    o_b = stick_breaking_attention(heads(q_b, SB_HEADS, False), heads(k_b, SB_HEADS, False),
                                   heads(v_b, SB_HEADS, False))
    o_a = o_a.transpose(0, 2, 1, 3).reshape(b, s, NSA_Q)
    o_b = o_b.transpose(0, 2, 1, 3).reshape(b, s, SB_W)
    gm = jax.nn.sigmoid(g_m)
    merged = gm[..., :D_MODEL] * (o_a @ w_branch_a) + gm[..., D_MODEL:] * (o_b @ w_branch_b)
    return merged @ w_out


def conv_ffn(h, w_up, conv_w, conv_b, w_down):
    u = h @ w_up
    c = u.shape[-1]
    u = lax.conv_general_dilated(u, conv_w[:, None, :].astype(u.dtype), window_strides=(1,),
                                 padding=[(CONV_W - 1, 0)], dimension_numbers=('NWC', 'WIO', 'NWC'),
                                 feature_group_count=c) + conv_b
    a, v = jnp.split(u, 2, axis=-1)
    return (jax.nn.silu(a) * v) @ w_down


def setup_inputs(seed: int = 0) -> dict:
    key = jax.random.key(seed)
    ks = jax.random.split(key, 24)
    L, D = DEPTH, D_MODEL

    def nrm(k, shape, scale):
        return jax.random.normal(k, shape, jnp.float32) * scale

    cmp_in = CMP_BLOCK * HEAD_DIM
    return {
        "x": nrm(ks[0], (BATCH, SEQ, D), 1.0),
        "w_in": nrm(ks[1], (L, D, IN_WIDTH), D ** -0.5),
        "cmp_pos_k": nrm(ks[2], (L, CMP_BLOCK, HEAD_DIM), 0.02),
        "cmp_w1_k": nrm(ks[3], (L, cmp_in, CMP_HIDDEN), cmp_in ** -0.5),
        "cmp_b1_k": nrm(ks[4], (L, CMP_HIDDEN), 0.02),
        "cmp_w2_k": nrm(ks[5], (L, CMP_HIDDEN, HEAD_DIM), CMP_HIDDEN ** -0.5),
        "cmp_pos_v": nrm(ks[6], (L, CMP_BLOCK, HEAD_DIM), 0.02),
        "cmp_w1_v": nrm(ks[7], (L, cmp_in, CMP_HIDDEN), cmp_in ** -0.5),
        "cmp_b1_v": nrm(ks[8], (L, CMP_HIDDEN), 0.02),
        "cmp_w2_v": nrm(ks[9], (L, CMP_HIDDEN, HEAD_DIM), CMP_HIDDEN ** -0.5),
        "w_branch_a": nrm(ks[10], (L, NSA_Q, D), NSA_Q ** -0.5),
        "w_branch_b": nrm(ks[11], (L, SB_W, D), SB_W ** -0.5),
        "w_out": nrm(ks[12], (L, D, D), D ** -0.5 * DEEPNORM_BETA),
        "ln_mix_g": 1.0 + nrm(ks[13], (L, D), 0.02),
        "ln_mix_b": nrm(ks[14], (L, D), 0.02),
        "w_up": nrm(ks[15], (L, D, 2 * D_FF), D ** -0.5),
        "conv_w": nrm(ks[16], (L, CONV_W, 2 * D_FF), CONV_W ** -0.5),
        "conv_b": nrm(ks[17], (L, 2 * D_FF), 0.02),
        "w_down": nrm(ks[18], (L, D_FF, D), D_FF ** -0.5 * DEEPNORM_BETA),
        "ln_ffn_g": 1.0 + nrm(ks[19], (L, D), 0.02),
        "ln_ffn_b": nrm(ks[20], (L, D), 0.02),
    }


def reference(x, w_in, cmp_pos_k, cmp_w1_k, cmp_b1_k, cmp_w2_k, cmp_pos_v, cmp_w1_v, cmp_b1_v, cmp_w2_v,
              w_branch_a, w_branch_b, w_out, ln_mix_g, ln_mix_b, w_up, conv_w, conv_b, w_down,
              ln_ffn_g, ln_ffn_b):
    cos, sin = rotary_tables(x.shape[1])
    for l in range(DEPTH):
        y = token_mixer(x, cos, sin, w_in[l], cmp_pos_k[l], cmp_w1_k[l], cmp_b1_k[l], cmp_w2_k[l],
                        cmp_pos_v[l], cmp_w1_v[l], cmp_b1_v[l], cmp_w2_v[l],
                        w_branch_a[l], w_branch_b[l], w_out[l])
        x = layer_norm(DEEPNORM_ALPHA * x + y, ln_mix_g[l], ln_mix_b[l])
        y = conv_ffn(x, w_up[l], conv_w[l], conv_b[l], w_down[l])
        x = layer_norm(DEEPNORM_ALPHA * x + y, ln_ffn_g[l], ln_ffn_b[l])
    return x
```

```python
import functools

import numpy as np
import jax
import jax.numpy as jnp
from jax import lax
from jax.experimental import pallas as pl
from jax.experimental.pallas import tpu as pltpu

D_MODEL = 1024
DEPTH = 4
HEAD_DIM = 64
NSA_HEADS = 8
NSA_KV_HEADS = 2
NSA_GROUP = NSA_HEADS // NSA_KV_HEADS
SB_HEADS = 8
CMP_BLOCK = 32
CMP_STRIDE = 16
CMP_HIDDEN = 128
SEL_BLOCK = 64
SEL_TOPK = 16
WINDOW = 512
ROPE_THETA = 500000.0
ROT_DIM = HEAD_DIM // 4
D_FF = 2816
CONV_W = 3
LN_EPS = 1e-5
NEG = -1e30
FORCE = 1e4
DEEPNORM_ALPHA = (2.0 * DEPTH) ** 0.25

NSA_Q = NSA_HEADS * HEAD_DIM
NSA_KV = NSA_KV_HEADS * HEAD_DIM
SB_W = SB_HEADS * HEAD_DIM
SPLIT_SIZES = (NSA_Q, NSA_KV, NSA_KV, NSA_KV, NSA_KV, NSA_KV, NSA_KV, 3 * NSA_HEADS, SB_W, SB_W, SB_W, 2 * D_MODEL)
SPLIT_POINTS = tuple(int(v) for v in np.cumsum(SPLIT_SIZES)[:-1])

LANES = 128
VMEM_LIMIT = 56 * 1024 * 1024

BF16 = jnp.bfloat16
F32 = jnp.float32

Q_TILE = 128
SEL_KEY_TILE = 512
SB_TILE = 256
N_SEL = 32


def _dot(a, b):
    return jnp.dot(a, b, preferred_element_type=F32)


def _dot_nt(a, b):
    return lax.dot_general(a, b, (((1,), (1,)), ((), ())), preferred_element_type=F32)


def _split_bf16(x):
    hi = x.astype(BF16)
    lo = (x - hi.astype(F32)).astype(BF16)
    return hi, lo


def _layer_norm(z, g, b):
    mu = jnp.mean(z, axis=-1, keepdims=True)
    zc = z - mu
    var = jnp.mean(zc * zc, axis=-1, keepdims=True)
    return zc * lax.rsqrt(var + LN_EPS) * g + b


_PROJ_OUT = (
    ("qa", NSA_Q, BF16, True, False, HEAD_DIM ** -0.5),
    ("kcmp", LANES, F32, True, False, 1.0),
    ("ksel", LANES, BF16, True, False, 1.0),
    ("kwin", LANES, BF16, True, False, 1.0),
    ("vcmp", LANES, F32, False, False, 1.0),
    ("vsel", LANES, BF16, False, False, 1.0),
    ("vwin", LANES, BF16, False, False, 1.0),
    ("qb", SB_W, BF16, False, False, HEAD_DIM ** -0.5),
    ("kb", SB_W, BF16, False, False, 1.0),
    ("vb", SB_W, BF16, False, False, 1.0),
    ("ga", LANES, F32, False, True, 1.0),
    ("gm", 2 * D_MODEL, F32, False, True, 1.0),
)
_PROJ_WIDTH = sum(o[1] for o in _PROJ_OUT)
PROJ_TM = 256


def _rope(acc, cos_t, sin_prev, sin_next):
    chunks = []
    for c in range(acc.shape[1] // LANES):
        xc = acc[:, c * LANES:(c + 1) * LANES]
        chunks.append(xc * cos_t + pltpu.roll(xc, ROT_DIM // 2, 1) * sin_prev
                      + pltpu.roll(xc, LANES - ROT_DIM // 2, 1) * sin_next)
    return chunks[0] if len(chunks) == 1 else jnp.concatenate(chunks, axis=1)


def _proj_kernel(x_ref, w_ref, cos_ref, sp_ref, sn_ref, *out_refs):
    x = x_ref[...]
    col = 0
    for (_, width, dtype, rotary, sigmoid, scale), o_ref in zip(_PROJ_OUT, out_refs):
        acc = _dot(x, w_ref[:, col:col + width])
        if rotary:
            acc = _rope(acc, cos_ref[...], sp_ref[...], sn_ref[...])
        if sigmoid:
            acc = jax.nn.sigmoid(acc)
        if scale != 1.0:
            acc = acc * scale
        o_ref[...] = acc.astype(dtype)
        col += width


def _in_proj(x_bf, w, cos_t, sin_prev, sin_next, seq):
    t = x_bf.shape[0]
    tm = PROJ_TM
    tiles_per_seq = seq // tm
    tab_spec = pl.BlockSpec((tm, LANES), lambda i: (i % tiles_per_seq, 0))
    return pl.pallas_call(
        _proj_kernel,
        grid=(t // tm,),
        in_specs=[
            pl.BlockSpec((tm, D_MODEL), lambda i: (i, 0)),
            pl.BlockSpec((D_MODEL, _PROJ_WIDTH), lambda i: (0, 0)),
            tab_spec, tab_spec, tab_spec,
        ],
        out_specs=[pl.BlockSpec((tm, o[1]), lambda i: (i, 0)) for o in _PROJ_OUT],
        out_shape=[jax.ShapeDtypeStruct((t, o[1]), o[2]) for o in _PROJ_OUT],
        compiler_params=pltpu.CompilerParams(
            dimension_semantics=("parallel",), vmem_limit_bytes=VMEM_LIMIT),
        name="in_proj",
    )(x_bf, w, cos_t, sin_prev, sin_next)


def _gelu_tanh(x):
    return 0.5 * x * (1.0 + jnp.tanh(np.sqrt(2.0 / np.pi).astype(np.float32) * (x + 0.044715 * (x * x * x))))


def _compress_kernel(r_ref, pos_ref, w1a_ref, w1b_ref, b1_ref, w2_ref, o_ref):
    r = r_ref[0]
    ra = (r + pos_ref[0:1, :]).astype(BF16)
    rb = (r + pos_ref[1:2, :]).astype(BF16)
    p1 = _dot(ra, w1a_ref[...])
    p2 = _dot(rb, w1b_ref[...])
    nrow = p2.shape[0]
    hid = _gelu_tanh(p1 + pltpu.roll(p2, nrow - 1, 0) + b1_ref[...])
    o_ref[0] = _dot(hid.astype(BF16), w2_ref[...]).astype(o_ref.dtype)


def _compress(r, pos, w1a, w1b, b1, w2):
    b, nrow, width = r.shape
    full = lambda a: pl.BlockSpec(a.shape, lambda i: (0,) * a.ndim)
    return pl.pallas_call(
        _compress_kernel,
        grid=(b,),
        in_specs=[pl.BlockSpec((1, nrow, width), lambda i: (i, 0, 0)),
                  full(pos), full(w1a), full(w1b), full(b1), full(w2)],
        out_specs=pl.BlockSpec((1, nrow, LANES), lambda i: (i, 0, 0)),
        out_shape=jax.ShapeDtypeStruct((b, nrow, LANES), BF16),
        compiler_params=pltpu.CompilerParams(
            dimension_semantics=("parallel",), vmem_limit_bytes=VMEM_LIMIT),
        name="compress",
    )(r, pos, w1a, w1b, b1, w2)


def _masked_softmax(s, mask):
    s = jnp.where(mask, s, NEG)
    m = jnp.max(s, axis=-1, keepdims=True)
    e = jnp.where(mask, jnp.exp(s - m), 0.0)
    return e / jnp.maximum(jnp.sum(e, axis=-1, keepdims=True), 1e-30)


def _tile_rows(a, n):
    return jnp.concatenate([a] * n, axis=0)


def _nsa_kernel(q_ref, kc_ref, vc_ref, ks_ref, vs_ref, kw_ref, vw_ref, ga_ref,
                ovt_ref, e_ref, x_ref, o_ref):
    qi = pl.program_id(1)
    t0 = qi * Q_TILE
    hq = NSA_GROUP * Q_TILE
    lane = lax.broadcasted_iota(jnp.int32, (1, LANES), 1)
    t_col = t0 + lax.broadcasted_iota(jnp.int32, (Q_TILE, 1), 0)
    t_rows = _tile_rows(t_col, NSA_GROUP)
    q_all = q_ref[0]

    ga_hi, ga_lo = _split_bf16(ga_ref[0])
    gates = _dot(ga_hi, x_ref[...]) + _dot(ga_lo, x_ref[...])

    j_col = lax.broadcasted_iota(jnp.int32, (N_SEL, 1), 0)
    t_lane = t0 + lax.broadcasted_iota(jnp.int32, (1, Q_TILE), 1)
    cur = t_lane // SEL_BLOCK
    forced = (j_col == 0) | (j_col == cur) | (j_col == cur - 1)
    valid = j_col * SEL_BLOCK <= t_lane

    res = []
    for g in range(NSA_KV_HEADS):
        in_group = (lane // HEAD_DIM) == g
        qs = jnp.concatenate(
            [jnp.where(in_group, q_all[:, h * LANES:(h + 1) * LANES], jnp.zeros((), BF16))
             for h in range(NSA_GROUP)], axis=0)

        s = _dot_nt(qs, kc_ref[0])
        c_lane = lax.broadcasted_iota(jnp.int32, (1, LANES), 1)
        m_cmp = (CMP_STRIDE * c_lane + CMP_BLOCK - 1) <= t_rows
        p = _masked_softmax(s, m_cmp)
        o_cmp = _dot(p.astype(BF16), vc_ref[0])

        psum = p[0:Q_TILE] + p[Q_TILE:2 * Q_TILE] + p[2 * Q_TILE:3 * Q_TILE] + p[3 * Q_TILE:4 * Q_TILE]
        ps_hi, ps_lo = _split_bf16(psum)
        score = _dot_nt(ovt_ref[...], ps_hi) + _dot_nt(ovt_ref[...], ps_lo)
        score = jnp.where(forced, FORCE, jnp.where(valid, score, -FORCE))
        rank = jnp.zeros((N_SEL, Q_TILE), F32)
        for jp in range(N_SEL):
            row = score[jp:jp + 1, :]
            ge = jnp.where(row >= score, 1.0, 0.0)
            gt = jnp.where(row > score, 1.0, 0.0)
            rank = rank + jnp.where(j_col > jp, ge, gt)
        sel_t = jnp.where(rank < SEL_TOPK, 1.0, 0.0)
        sel_t = jnp.concatenate([sel_t, jnp.zeros((LANES - N_SEL, Q_TILE), F32)], axis=0)
        sel = sel_t.T.astype(BF16)

        def sel_step(kt, carry):
            m_i, l_i, acc = carry
            k0 = pl.multiple_of(kt * SEL_KEY_TILE, SEL_KEY_TILE)
            k = ks_ref[0, pl.ds(k0, SEL_KEY_TILE), :]
            v = vs_ref[0, pl.ds(k0, SEL_KEY_TILE), :]
            sc = _dot_nt(qs, k)
            picked = _dot(sel, e_ref[:, pl.ds(k0, SEL_KEY_TILE)]) > 0.5
            kpos = k0 + lax.broadcasted_iota(jnp.int32, (1, SEL_KEY_TILE), 1)
            mask = _tile_rows(picked & (kpos <= t_col), NSA_GROUP)
            sc = jnp.where(mask, sc, NEG)
            m_new = jnp.maximum(m_i, jnp.max(sc, axis=-1, keepdims=True))
            e = jnp.where(mask, jnp.exp(sc - m_new), 0.0)
            alpha = jnp.exp(m_i - m_new)
            l_new = alpha * l_i + jnp.sum(e, axis=-1, keepdims=True)
            acc_new = alpha * acc + _dot(e.astype(BF16), v)
            return m_new, l_new, acc_new

        n_tiles = (t0 + Q_TILE + SEL_KEY_TILE - 1) // SEL_KEY_TILE
        m_i, l_i, acc = lax.fori_loop(
            0, n_tiles, sel_step,
            (jnp.full((hq, 1), NEG, F32), jnp.zeros((hq, 1), F32), jnp.zeros((hq, LANES), F32)))
        o_sel = acc / jnp.maximum(l_i, 1e-30)

        w0 = pl.multiple_of(jnp.maximum(t0 - WINDOW, 0), Q_TILE)
        wlen = WINDOW + Q_TILE
        kw = kw_ref[0, pl.ds(w0, wlen), :]
        vw = vw_ref[0, pl.ds(w0, wlen), :]
        sw = _dot_nt(qs, kw)
        diff = t_rows - (w0 + lax.broadcasted_iota(jnp.int32, (1, wlen), 1))
        pw = _masked_softmax(sw, (diff >= 0) & (diff < WINDOW))
        o_win = _dot(pw.astype(BF16), vw)

        res.append((o_cmp, o_sel, o_win))

    for h in range(NSA_GROUP):
        rows = slice(h * Q_TILE, (h + 1) * Q_TILE)
        mixed = []
        for g in range(NSA_KV_HEADS):
            acc = None
            for r in range(3):
                term = gates[:, r * NSA_Q + h * LANES: r * NSA_Q + (h + 1) * LANES] * res[g][r][rows]
                acc = term if acc is None else acc + term
            mixed.append(acc)
        o_ref[0, :, h * LANES:(h + 1) * LANES] = jnp.where(lane < HEAD_DIM, mixed[0], mixed[1]).astype(o_ref.dtype)


def _nsa_attention(qa, kc, vc, ksel, vsel, kwin, vwin, ga, ovt, expand, gate_x):
    b, s, _ = qa.shape
    per_q = lambda w: pl.BlockSpec((1, Q_TILE, w), lambda i, j: (i, j, 0))
    per_b = lambda a: pl.BlockSpec((1,) + a.shape[1:], lambda i, j: (i, 0, 0))
    full = lambda a: pl.BlockSpec(a.shape, lambda i, j: (0,) * a.ndim)
    return pl.pallas_call(
        _nsa_kernel,
        grid=(b, s // Q_TILE),
        in_specs=[per_q(NSA_Q), per_b(kc), per_b(vc), per_b(ksel), per_b(vsel), per_b(kwin), per_b(vwin),
                  per_q(LANES), full(ovt), full(expand), full(gate_x)],
        out_specs=per_q(NSA_Q),
        out_shape=jax.ShapeDtypeStruct((b, s, NSA_Q), BF16),
        compiler_params=pltpu.CompilerParams(
            dimension_semantics=("parallel", "parallel"), vmem_limit_bytes=VMEM_LIMIT),
        name="nsa_attention",
    )(qa, kc, vc, ksel, vsel, kwin, vwin, ga, ovt, expand, gate_x)


def _sb_kernel(q_ref, k_ref, v_ref, u_ref, o_ref):
    qi = pl.program_id(2)
    t0 = qi * SB_TILE
    lane = lax.broadcasted_iota(jnp.int32, (1, LANES), 1)
    t_col = t0 + lax.broadcasted_iota(jnp.int32, (SB_TILE, 1), 0)
    q = q_ref[0]
    outs = []
    for hl in range(2):
        qm = jnp.where((lane // HEAD_DIM) == hl, q, jnp.zeros((), BF16))

        def step(i, carry):
            run, acc = carry
            k0 = pl.multiple_of((qi - i) * SB_TILE, SB_TILE)
            k = k_ref[0, pl.ds(k0, SB_TILE), :]
            v = v_ref[0, pl.ds(k0, SB_TILE), :]
            z = _dot_nt(qm, k)
            mask = (k0 + lax.broadcasted_iota(jnp.int32, (1, SB_TILE), 1)) < t_col
            sp = jnp.log1p(jnp.exp(-jnp.abs(z)))
            log_beta = jnp.minimum(z, 0.0) - sp
            log_fail = jnp.where(mask, -jnp.maximum(z, 0.0) - sp, 0.0)
            lf_hi, lf_lo = _split_bf16(log_fail)
            suffix = _dot(lf_hi, u_ref[...]) + _dot(lf_lo, u_ref[...]) + run
            a = jnp.where(mask, jnp.exp(log_beta + suffix), 0.0)
            run = run + jnp.sum(log_fail, axis=-1, keepdims=True)
            return run, acc + _dot(a.astype(BF16), v)

        _, acc = lax.fori_loop(0, qi + 1, step,
                               (jnp.zeros((SB_TILE, 1), F32), jnp.zeros((SB_TILE, LANES), F32)))
        outs.append(acc)
    o_ref[0] = jnp.where(lane < HEAD_DIM, outs[0], outs[1]).astype(o_ref.dtype)


def _sb_attention(qb, kb, vb, upper):
    b, s, w = qb.shape
    return pl.pallas_call(
        _sb_kernel,
        grid=(b, w // LANES, s // SB_TILE),
        in_specs=[pl.BlockSpec((1, SB_TILE, LANES), lambda i, j, t: (i, t, j)),
                  pl.BlockSpec((1, s, LANES), lambda i, j, t: (i, 0, j)),
                  pl.BlockSpec((1, s, LANES), lambda i, j, t: (i, 0, j)),
                  pl.BlockSpec(upper.shape, lambda i, j, t: (0, 0))],
        out_specs=pl.BlockSpec((1, SB_TILE, LANES), lambda i, j, t: (i, t, j)),
        out_shape=jax.ShapeDtypeStruct((b, s, w), BF16),
        compiler_params=pltpu.CompilerParams(
            dimension_semantics=("parallel", "parallel", "parallel"), vmem_limit_bytes=VMEM_LIMIT),
        name="sb_attention",
    )(qb, kb, vb, upper)


MERGE_TM = 256


def _merge_kernel(oa_ref, ob_ref, gm_ref, x_ref, wa_ref, wb_ref, wo_ref, g_ref, b_ref, y_ref, ybf_ref):
    ya = _dot(oa_ref[...], wa_ref[...])
    yb = _dot(ob_ref[...], wb_ref[...])
    merged = gm_ref[:, :D_MODEL] * ya + gm_ref[:, D_MODEL:] * yb
    y = _dot(merged.astype(BF16), wo_ref[...])
    out = _layer_norm(DEEPNORM_ALPHA * x_ref[...] + y, g_ref[...], b_ref[...])
    y_ref[...] = out
    ybf_ref[...] = out.astype(BF16)


def _merge(oa, ob, gm, x, wa, wb, wo, g, b):
    t = x.shape[0]
    tm = MERGE_TM
    row = lambda w: pl.BlockSpec((tm, w), lambda i: (i, 0))
    full = lambda a: pl.BlockSpec(a.shape, lambda i: (0, 0))
    return pl.pallas_call(
        _merge_kernel,
        grid=(t // tm,),
        in_specs=[row(NSA_Q), row(SB_W), row(2 * D_MODEL), row(D_MODEL),
                  full(wa), full(wb), full(wo), full(g), full(b)],
        out_specs=[row(D_MODEL), row(D_MODEL)],
        out_shape=[jax.ShapeDtypeStruct((t, D_MODEL), F32), jax.ShapeDtypeStruct((t, D_MODEL), BF16)],
        compiler_params=pltpu.CompilerParams(
            dimension_semantics=("parallel",), vmem_limit_bytes=VMEM_LIMIT),
        name="merge_out_ln",
    )(oa, ob, gm, x, wa, wb, wo, g, b)


FFN_TM = 512
FFN_TN = 256
HALO = 16


def _ffn_kernel(h_ref, halo_ref, x_ref, wua_ref, wuv_ref, cwa_ref, cwv_ref, cba_ref, cbv_ref, wd_ref,
                g_ref, b_ref, y_ref, ybf_ref, acc_ref, ua_ref, uv_ref, *, tiles_per_seq):
    i = pl.program_id(0)
    j = pl.program_id(1)
    has_prev = (i % tiles_per_seq) != 0

    def conv_branch(w_ref, cw_ref, cb_ref, buf):
        u_prev = _dot(halo_ref[...], w_ref[...])
        buf[0:HALO, :] = jnp.where(has_prev, u_prev, 0.0)
        u = _dot(h_ref[...], w_ref[...])
        buf[HALO:, :] = u
        tm = u.shape[0]
        return (cw_ref[0:1, :] * buf[HALO - 2:HALO - 2 + tm, :] + cw_ref[1:2, :] * buf[HALO - 1:HALO - 1 + tm, :]
                + cw_ref[2:3, :] * u + cb_ref[...])

    a = conv_branch(wua_ref, cwa_ref, cba_ref, ua_ref)
    v = conv_branch(wuv_ref, cwv_ref, cbv_ref, uv_ref)
    act = (a * jax.nn.sigmoid(a)) * v
    part = _dot(act.astype(BF16), wd_ref[...])

    @pl.when(j == 0)
    def _():
        acc_ref[...] = part

    @pl.when(j > 0)
    def _():
        acc_ref[...] += part

    @pl.when(j == pl.num_programs(1) - 1)
    def _():
        out = _layer_norm(DEEPNORM_ALPHA * x_ref[...] + acc_ref[...], g_ref[...], b_ref[...])
        y_ref[...] = out
        ybf_ref[...] = out.astype(BF16)


def _conv_ffn(h_bf, x, w_up, conv_w, conv_b, w_down, g, b, seq):
    t = x.shape[0]
    tm, tn = FFN_TM, FFN_TN
    nj = D_FF // tn
    tiles_per_seq = seq // tm
    row = lambda w: pl.BlockSpec((tm, w), lambda i, j: (i, 0))
    col_a = lambda r: pl.BlockSpec((r, tn), lambda i, j: (0, j))
    col_v = lambda r: pl.BlockSpec((r, tn), lambda i, j: (0, nj + j))
    full = lambda a: pl.BlockSpec(a.shape, lambda i, j: (0, 0))
    return pl.pallas_call(
        functools.partial(_ffn_kernel, tiles_per_seq=tiles_per_seq),
        grid=(t // tm, nj),
        in_specs=[row(D_MODEL),
                  pl.BlockSpec((HALO, D_MODEL), lambda i, j: (jnp.maximum(i * (tm // HALO) - 1, 0), 0)),
                  row(D_MODEL),
                  col_a(D_MODEL), col_v(D_MODEL), col_a(CONV_W), col_v(CONV_W), col_a(1), col_v(1),
                  pl.BlockSpec((tn, D_MODEL), lambda i, j: (j, 0)),
                  full(g), full(b)],
        out_specs=[row(D_MODEL), row(D_MODEL)],
        out_shape=[jax.ShapeDtypeStruct((t, D_MODEL), F32), jax.ShapeDtypeStruct((t, D_MODEL), BF16)],
        scratch_shapes=[pltpu.VMEM((tm, D_MODEL), F32),
                        pltpu.VMEM((tm + HALO, tn), F32), pltpu.VMEM((tm + HALO, tn), F32)],
        compiler_params=pltpu.CompilerParams(
            dimension_semantics=("parallel", "arbitrary"), vmem_limit_bytes=VMEM_LIMIT),
        name="conv_ffn_ln",
    )(h_bf, h_bf, x, w_up, w_up, conv_w, conv_w, conv_b, conv_b, w_down, g, b)


def _rope_tables(seq):
    inv_freq = ROPE_THETA ** (-np.arange(0, ROT_DIM, 2, dtype=np.float32) / ROT_DIM)
    ang = jnp.arange(seq, dtype=F32)[:, None] * jnp.asarray(inv_freq, F32)[None, :]
    cos, sin = jnp.cos(ang), jnp.sin(ang)
    half = ROT_DIM // 2
    ones = jnp.ones((seq, HEAD_DIM - ROT_DIM), F32)
    zeros = jnp.zeros((seq, HEAD_DIM - ROT_DIM), F32)
    zh = jnp.zeros((seq, half), F32)
    cos_t = jnp.concatenate([cos, cos, ones], axis=1)
    sin_prev = jnp.concatenate([zh, sin, zeros], axis=1)
    sin_next = jnp.concatenate([-sin, zh, zeros], axis=1)
    rep = LANES // HEAD_DIM
    return tuple(jnp.tile(a, (1, rep)) for a in (cos_t, sin_prev, sin_next))


_QA_PERM = np.concatenate([np.concatenate([np.arange(h * HEAD_DIM, (h + 1) * HEAD_DIM),
                                           np.arange((NSA_GROUP + h) * HEAD_DIM, (NSA_GROUP + h + 1) * HEAD_DIM)])
                           for h in range(NSA_GROUP)])


def _selection_overlap_t(seq):
    n_cmp = (seq - CMP_BLOCK) // CMP_STRIDE + 1
    n_sel = seq // SEL_BLOCK
    cs = np.arange(n_cmp) * CMP_STRIDE
    ce = cs + CMP_BLOCK
    ss = np.arange(n_sel) * SEL_BLOCK
    se = ss + SEL_BLOCK
    ov = np.clip(np.minimum(ce[:, None], se[None, :]) - np.maximum(cs[:, None], ss[None, :]), 0, None) / CMP_BLOCK
    out = np.zeros((n_sel, seq // CMP_STRIDE), np.float32)
    out[:, :n_cmp] = ov.T
    return jnp.asarray(out, BF16)


def _block_expand(seq):
    e = np.zeros((LANES, seq), np.float32)
    e[np.arange(seq) // SEL_BLOCK, np.arange(seq)] = 1.0
    return jnp.asarray(e, BF16)


def _gate_expand():
    x = np.zeros((LANES, 3 * NSA_Q), np.float32)
    for col in range(NSA_Q):
        hh = _QA_PERM[col] // HEAD_DIM
        for r in range(3):
            x[hh * 3 + r, r * NSA_Q + col] = 1.0
    return jnp.asarray(x, BF16)


def _strict_upper(n):
    return jnp.asarray(np.arange(n)[:, None] > np.arange(n)[None, :], BF16)


def _prep_in_proj(w):
    q_a, kc, vc, ks, vs, kw, vw, g_a, q_b, k_b, v_b, g_m = jnp.split(w, SPLIT_POINTS, axis=-1)
    g_a = jnp.pad(g_a, ((0, 0), (0, LANES - g_a.shape[1])))
    cols = [q_a[:, _QA_PERM], kc, ks, kw, vc, vs, vw, q_b, k_b, v_b, g_a, g_m]
    return jnp.concatenate(cols, axis=1).astype(BF16)


def _prep_compress(pos, w1, b1, w2):
    half = CMP_BLOCK // 2
    pos2 = pos.reshape(2, half, 1, HEAD_DIM)
    pos_t = jnp.broadcast_to(pos2, (2, half, NSA_KV_HEADS, HEAD_DIM)).reshape(2, half * NSA_KV)
    w1r = w1.reshape(2, half, HEAD_DIM, CMP_HIDDEN)
    eye = jnp.eye(NSA_KV_HEADS, dtype=w1.dtype)
    w1x = (w1r[:, :, None, :, None, :] * eye[None, None, :, None, :, None]).reshape(
        2, half * NSA_KV, NSA_KV_HEADS * CMP_HIDDEN)
    b1t = jnp.tile(b1.reshape(1, CMP_HIDDEN), (1, NSA_KV_HEADS))
    w2x = (w2[None, :, None, :] * eye[:, None, :, None]).reshape(NSA_KV_HEADS * CMP_HIDDEN, NSA_KV)
    return pos_t, w1x[0].astype(BF16), w1x[1].astype(BF16), b1t, w2x.astype(BF16)


def kernel(x, w_in, cmp_pos_k, cmp_w1_k, cmp_b1_k, cmp_w2_k, cmp_pos_v, cmp_w1_v, cmp_b1_v, cmp_w2_v,
           w_branch_a, w_branch_b, w_out, ln_mix_g, ln_mix_b, w_up, conv_w, conv_b, w_down,
           ln_ffn_g, ln_ffn_b):
    b, s, d = x.shape
    t = b * s
    assert d == D_MODEL and s // SEL_BLOCK == N_SEL and s // CMP_STRIDE == LANES
    cos_t, sin_prev, sin_next = _rope_tables(s)
    ovt = _selection_overlap_t(s)
    expand = _block_expand(s)
    gate_x = _gate_expand()
    upper = _strict_upper(SB_TILE)

    xf = x.reshape(t, d)
    xb = xf.astype(BF16)
    for l in range(DEPTH):
        outs = _in_proj(xb, _prep_in_proj(w_in[l]), cos_t, sin_prev, sin_next, s)
        qa, kcmp, ksel, kwin, vcmp, vsel, vwin, qb, kb, vb, ga, gm = outs
        rows = s // CMP_STRIDE
        kc = _compress(kcmp.reshape(b, rows, CMP_STRIDE * LANES),
                       *_prep_compress(cmp_pos_k[l], cmp_w1_k[l], cmp_b1_k[l], cmp_w2_k[l]))
        vc = _compress(vcmp.reshape(b, rows, CMP_STRIDE * LANES),
                       *_prep_compress(cmp_pos_v[l], cmp_w1_v[l], cmp_b1_v[l], cmp_w2_v[l]))
        r3 = lambda a: a.reshape(b, s, a.shape[-1])
        oa = _nsa_attention(r3(qa), kc, vc, r3(ksel), r3(vsel), r3(kwin), r3(vwin), r3(ga),
                            ovt, expand, gate_x)
        ob = _sb_attention(r3(qb), r3(kb), r3(vb), upper)
        xf, xb = _merge(oa.reshape(t, NSA_Q), ob.reshape(t, SB_W), gm, xf,
                        w_branch_a[l][_QA_PERM, :].astype(BF16), w_branch_b[l].astype(BF16),
                        w_out[l].astype(BF16), ln_mix_g[l].reshape(1, d), ln_mix_b[l].reshape(1, d))
        xf, xb = _conv_ffn(xb, xf, w_up[l].astype(BF16), conv_w[l], conv_b[l].reshape(1, -1),
                           w_down[l].astype(BF16), ln_ffn_g[l].reshape(1, d), ln_ffn_b[l].reshape(1, d), s)
    return xf.reshape(b, s, d)
```

```python
import functools

import numpy as np
import jax
import jax.numpy as jnp
from jax import lax
from jax.experimental import pallas as pl
from jax.experimental.pallas import tpu as pltpu

D_MODEL = 1024
DEPTH = 4
HEAD_DIM = 64
NSA_HEADS = 8
NSA_KV_HEADS = 2
NSA_GROUP = NSA_HEADS // NSA_KV_HEADS
SB_HEADS = 8
CMP_BLOCK = 32
CMP_STRIDE = 16
CMP_HIDDEN = 128
SEL_BLOCK = 64
SEL_TOPK = 16
WINDOW = 512
ROPE_THETA = 500000.0
ROT_DIM = HEAD_DIM // 4
D_FF = 2816
CONV_W = 3
LN_EPS = 1e-5
NEG = -1e30
FORCE = 1e4
DEEPNORM_ALPHA = (2.0 * DEPTH) ** 0.25

NSA_Q = NSA_HEADS * HEAD_DIM
NSA_KV = NSA_KV_HEADS * HEAD_DIM
SB_W = SB_HEADS * HEAD_DIM
SPLIT_SIZES = (NSA_Q, NSA_KV, NSA_KV, NSA_KV, NSA_KV, NSA_KV, NSA_KV, 3 * NSA_HEADS, SB_W, SB_W, SB_W, 2 * D_MODEL)
SPLIT_POINTS = tuple(int(v) for v in np.cumsum(SPLIT_SIZES)[:-1])

LANES = 128
VMEM_LIMIT = 56 * 1024 * 1024

BF16 = jnp.bfloat16
F32 = jnp.float32

Q_TILE = 128
SEL_KEY_TILE = 512
SB_TILE = 256
N_SEL = 32


def _dot(a, b):
    return jnp.dot(a, b, preferred_element_type=F32)


def _dot_nt(a, b):
    return lax.dot_general(a, b, (((1,), (1,)), ((), ())), preferred_element_type=F32)


def _split_bf16(x):
    hi = x.astype(BF16)
    lo = (x - hi.astype(F32)).astype(BF16)
    return hi, lo


def _layer_norm(z, g, b):
    mu = jnp.mean(z, axis=-1, keepdims=True)
    zc = z - mu
    var = jnp.mean(zc * zc, axis=-1, keepdims=True)
    return zc * lax.rsqrt(var + LN_EPS) * g + b


def _resident(a):
    return pl.BlockSpec(a.shape, lambda *_: (0,) * a.ndim, pipeline_mode=pl.Buffered(1))


_PROJ_OUT = (
    ("qa", NSA_Q, BF16, True, False, HEAD_DIM ** -0.5),
    ("kcmp", LANES, F32, True, False, 1.0),
    ("ksel", LANES, BF16, True, False, 1.0),
    ("kwin", LANES, BF16, True, False, 1.0),
    ("vcmp", LANES, F32, False, False, 1.0),
    ("vsel", LANES, BF16, False, False, 1.0),
    ("vwin", LANES, BF16, False, False, 1.0),
    ("qb", SB_W, BF16, False, False, HEAD_DIM ** -0.5),
    ("kb", SB_W, BF16, False, False, 1.0),
    ("vb", SB_W, BF16, False, False, 1.0),
    ("ga", LANES, F32, False, True, 1.0),
    ("gm", 2 * D_MODEL, F32, False, True, 1.0),
)
_PROJ_WIDTH = sum(o[1] for o in _PROJ_OUT)
PROJ_TM = 256


def _rope(acc, cos_t, sin_prev, sin_next):
    chunks = []
    for c in range(acc.shape[1] // LANES):
        xc = acc[:, c * LANES:(c + 1) * LANES]
        chunks.append(xc * cos_t + pltpu.roll(xc, ROT_DIM // 2, 1) * sin_prev
                      + pltpu.roll(xc, LANES - ROT_DIM // 2, 1) * sin_next)
    return chunks[0] if len(chunks) == 1 else jnp.concatenate(chunks, axis=1)


def _proj_kernel(x_ref, w_ref, cos_ref, sp_ref, sn_ref, *out_refs):
    x = x_ref[...]
    col = 0
    for (_, width, dtype, rotary, sigmoid, scale), o_ref in zip(_PROJ_OUT, out_refs):
        acc = _dot(x, w_ref[:, col:col + width])
        if rotary:
            acc = _rope(acc, cos_ref[...], sp_ref[...], sn_ref[...])
        if sigmoid:
            acc = jax.nn.sigmoid(acc)
        if scale != 1.0:
            acc = acc * scale
        o_ref[...] = acc.astype(dtype)
        col += width


def _in_proj(x_bf, w, cos_t, sin_prev, sin_next, seq):
    t = x_bf.shape[0]
    tm = PROJ_TM
    tiles_per_seq = seq // tm
    tab_spec = pl.BlockSpec((tm, LANES), lambda i: (i % tiles_per_seq, 0))
    return pl.pallas_call(
        _proj_kernel,
        grid=(t // tm,),
        in_specs=[
            pl.BlockSpec((tm, D_MODEL), lambda i: (i, 0)),
            _resident(w),
            tab_spec, tab_spec, tab_spec,
        ],
        out_specs=[pl.BlockSpec((tm, o[1]), lambda i: (i, 0)) for o in _PROJ_OUT],
        out_shape=[jax.ShapeDtypeStruct((t, o[1]), o[2]) for o in _PROJ_OUT],
        compiler_params=pltpu.CompilerParams(
            dimension_semantics=("parallel",), vmem_limit_bytes=VMEM_LIMIT),
        name="in_proj",
    )(x_bf, w, cos_t, sin_prev, sin_next)


def _gelu_tanh(x):
    return 0.5 * x * (1.0 + jnp.tanh(np.sqrt(2.0 / np.pi).astype(np.float32) * (x + 0.044715 * (x * x * x))))


def _compress_kernel(r_ref, pos_ref, w1a_ref, w1b_ref, b1_ref, w2_ref, o_ref):
    r = r_ref[0]
    ra = (r + pos_ref[0:1, :]).astype(BF16)
    rb = (r + pos_ref[1:2, :]).astype(BF16)
    p1 = _dot(ra, w1a_ref[...])
    p2 = _dot(rb, w1b_ref[...])
    nrow = p2.shape[0]
    hid = _gelu_tanh(p1 + pltpu.roll(p2, nrow - 1, 0) + b1_ref[...])
    o_ref[0] = _dot(hid.astype(BF16), w2_ref[...]).astype(o_ref.dtype)


def _compress(r, pos, w1a, w1b, b1, w2):
    b, nrow, width = r.shape
    full = _resident
    return pl.pallas_call(
        _compress_kernel,
        grid=(b,),
        in_specs=[pl.BlockSpec((1, nrow, width), lambda i: (i, 0, 0)),
                  full(pos), full(w1a), full(w1b), full(b1), full(w2)],
        out_specs=pl.BlockSpec((1, nrow, LANES), lambda i: (i, 0, 0)),
        out_shape=jax.ShapeDtypeStruct((b, nrow, LANES), BF16),
        compiler_params=pltpu.CompilerParams(
            dimension_semantics=("parallel",), vmem_limit_bytes=VMEM_LIMIT),
        name="compress",
    )(r, pos, w1a, w1b, b1, w2)


def _masked_softmax(s, mask):
    s = jnp.where(mask, s, NEG)
    m = jnp.max(s, axis=-1, keepdims=True)
    e = jnp.where(mask, jnp.exp(s - m), 0.0)
    return e / jnp.maximum(jnp.sum(e, axis=-1, keepdims=True), 1e-30)


def _tile_rows(a, n):
    return jnp.concatenate([a] * n, axis=0)


def _nsa_kernel(q_ref, kc_ref, vc_ref, ks_ref, vs_ref, kw_ref, vw_ref, ga_ref,
                ovt_ref, et_ref, x_ref, o_ref):
    qi = pl.program_id(1)
    t0 = qi * Q_TILE
    hq = NSA_GROUP * Q_TILE
    nrows = NSA_HEADS * Q_TILE
    lane = lax.broadcasted_iota(jnp.int32, (1, LANES), 1)
    t_col = t0 + lax.broadcasted_iota(jnp.int32, (Q_TILE, 1), 0)
    t_rows = _tile_rows(t_col, NSA_HEADS)
    q_all = q_ref[0]

    ga_hi, ga_lo = _split_bf16(ga_ref[0])
    gates = _dot(ga_hi, x_ref[...]) + _dot(ga_lo, x_ref[...])

    j_col = lax.broadcasted_iota(jnp.int32, (N_SEL, 1), 0)
    t_lane = t0 + lax.broadcasted_iota(jnp.int32, (1, Q_TILE), 1)
    cur = t_lane // SEL_BLOCK
    forced = (j_col == 0) | (j_col == cur) | (j_col == cur - 1)
    valid = j_col * SEL_BLOCK <= t_lane

    w0 = pl.multiple_of(jnp.maximum(t0 - WINDOW, 0), Q_TILE)
    wlen = WINDOW + Q_TILE
    diff = t_col - (w0 + lax.broadcasted_iota(jnp.int32, (1, wlen), 1))
    win_bias = _tile_rows(jnp.where((diff >= 0) & (diff < WINDOW), 0.0, NEG), NSA_HEADS)
    kw = kw_ref[0, pl.ds(w0, wlen), :]
    vw = vw_ref[0, pl.ds(w0, wlen), :]

    last = qi // (SEL_KEY_TILE // Q_TILE)
    k_last = pl.multiple_of(last * SEL_KEY_TILE, SEL_KEY_TILE)
    causal = _tile_rows((k_last + lax.broadcasted_iota(jnp.int32, (1, SEL_KEY_TILE), 1)) <= t_col, NSA_HEADS)

    qs = jnp.concatenate(
        [jnp.where((lane // HEAD_DIM) == g, q_all[:, h * LANES:(h + 1) * LANES], jnp.zeros((), BF16))
         for g in range(NSA_KV_HEADS) for h in range(NSA_GROUP)], axis=0)

    s = _dot_nt(qs, kc_ref[0])
    m_cmp = (CMP_STRIDE * lane + CMP_BLOCK - 1) <= t_rows
    p = _masked_softmax(s, m_cmp)
    o_cmp = _dot(p.astype(BF16), vc_ref[0])

    drops = []
    for g in range(NSA_KV_HEADS):
        pg = p[g * hq:(g + 1) * hq]
        psum = pg[0:Q_TILE] + pg[Q_TILE:2 * Q_TILE] + pg[2 * Q_TILE:3 * Q_TILE] + pg[3 * Q_TILE:4 * Q_TILE]
        ps_hi, ps_lo = _split_bf16(psum)
        score = _dot_nt(ovt_ref[...], ps_hi) + _dot_nt(ovt_ref[...], ps_lo)
        score = jnp.where(forced, FORCE, jnp.where(valid, score, -FORCE))
        rank = jnp.zeros((N_SEL, Q_TILE), F32)
        for jp in range(N_SEL):
            row = score[jp:jp + 1, :]
            ge = jnp.where(row >= score, 1.0, 0.0)
            gt = jnp.where(row > score, 1.0, 0.0)
            rank = rank + jnp.where(j_col > jp, ge, gt)
        drop_t = jnp.where(rank < SEL_TOPK, 0.0, NEG)
        drop_t = jnp.concatenate([drop_t, jnp.zeros((LANES - N_SEL, Q_TILE), F32)], axis=0)
        drop = drop_t.T.astype(BF16)
        drops.append(_tile_rows(drop, NSA_GROUP))
    q_aug = jnp.concatenate([qs, jnp.concatenate(drops, axis=0)], axis=1)

    def sel_tile(k0, carry, mask):
        m_i, l_i, acc = carry
        k_aug = jnp.concatenate([ks_ref[0, pl.ds(k0, SEL_KEY_TILE), :],
                                 et_ref[pl.ds(k0, SEL_KEY_TILE), :]], axis=1)
        v = vs_ref[0, pl.ds(k0, SEL_KEY_TILE), :]
        sc = _dot_nt(q_aug, k_aug)
        if mask is not None:
            sc = jnp.where(mask, sc, NEG)
        m_new = jnp.maximum(m_i, jnp.max(sc, axis=-1, keepdims=True))
        e = jnp.exp(sc - m_new)
        alpha = jnp.exp(m_i - m_new)
        l_new = alpha * l_i + jnp.sum(e, axis=-1, keepdims=True)
        acc_new = alpha * acc + _dot(e.astype(BF16), v)
        return m_new, l_new, acc_new

    carry = lax.fori_loop(
        0, last, lambda kt, c: sel_tile(pl.multiple_of(kt * SEL_KEY_TILE, SEL_KEY_TILE), c, None),
        (jnp.full((nrows, 1), NEG, F32), jnp.zeros((nrows, 1), F32), jnp.zeros((nrows, LANES), F32)))
    _, l_i, acc = sel_tile(k_last, carry, causal)
    o_sel = acc / jnp.maximum(l_i, 1e-30)

    sw = _dot_nt(qs, kw) + win_bias
    ew = jnp.exp(sw - jnp.max(sw, axis=-1, keepdims=True))
    o_win = _dot(ew.astype(BF16), vw) / jnp.maximum(jnp.sum(ew, axis=-1, keepdims=True), 1e-30)

    branches = (o_cmp, o_sel, o_win)
    for h in range(NSA_GROUP):
        mixed = []
        for g in range(NSA_KV_HEADS):
            r0 = (g * NSA_GROUP + h) * Q_TILE
            acc = None
            for r in range(3):
                term = gates[:, r * NSA_Q + h * LANES: r * NSA_Q + (h + 1) * LANES] * branches[r][r0:r0 + Q_TILE]
                acc = term if acc is None else acc + term
            mixed.append(acc)
        o_ref[0, :, h * LANES:(h + 1) * LANES] = jnp.where(lane < HEAD_DIM, mixed[0], mixed[1]).astype(o_ref.dtype)


def _nsa_attention(qa, kc, vc, ksel, vsel, kwin, vwin, ga, ovt, expand, gate_x):
    b, s, _ = qa.shape
    per_q = lambda w: pl.BlockSpec((1, Q_TILE, w), lambda i, j: (i, j, 0))
    per_b = lambda a: pl.BlockSpec((1,) + a.shape[1:], lambda i, j: (i, 0, 0))
    full = _resident
    return pl.pallas_call(
        _nsa_kernel,
        grid=(b, s // Q_TILE),
        in_specs=[per_q(NSA_Q), per_b(kc), per_b(vc), per_b(ksel), per_b(vsel), per_b(kwin), per_b(vwin),
                  per_q(LANES), full(ovt), full(expand), full(gate_x)],
        out_specs=per_q(NSA_Q),
        out_shape=jax.ShapeDtypeStruct((b, s, NSA_Q), BF16),
        compiler_params=pltpu.CompilerParams(
            dimension_semantics=("parallel", "parallel"), vmem_limit_bytes=VMEM_LIMIT),
        name="nsa_attention",
    )(qa, kc, vc, ksel, vsel, kwin, vwin, ga, ovt, expand, gate_x)


SB_SKIP = 104.0


SB_PAIRS = 2


def _sb_kernel(q_ref, k_ref, v_ref, u_ref, o_ref):
    qi = pl.program_id(2)
    t0 = qi * SB_TILE
    lane = lax.broadcasted_iota(jnp.int32, (1, LANES), 1)
    t_col = t0 + lax.broadcasted_iota(jnp.int32, (SB_TILE, 1), 0)
    diag_mask = (t0 + lax.broadcasted_iota(jnp.int32, (1, SB_TILE), 1)) < t_col
    diag_mask = _tile_rows(diag_mask, 2)
    qms = []
    for p in range(SB_PAIRS):
        q = q_ref[0, :, p * LANES:(p + 1) * LANES]
        qms.append(jnp.concatenate(
            [jnp.where((lane // HEAD_DIM) == hl, q, jnp.zeros((), BF16)) for hl in range(2)], axis=0))

    def tile(p, k0, run, acc, mask):
        k = k_ref[0, pl.ds(k0, SB_TILE), p * LANES:(p + 1) * LANES]
        v = v_ref[0, pl.ds(k0, SB_TILE), p * LANES:(p + 1) * LANES]
        z = _dot_nt(qms[p], k)
        sp = jnp.log(1.0 + jnp.exp(-jnp.abs(z)))
        log_beta = jnp.minimum(z, 0.0) - sp
        nfail = jnp.maximum(z, 0.0) + sp
        if mask is not None:
            nfail = jnp.where(mask, nfail, 0.0)
        hi, lo = _split_bf16(nfail)
        suffix = _dot(jnp.concatenate([hi, lo], axis=1), u_ref[...])
        a = jnp.exp(log_beta + suffix + run)
        if mask is not None:
            a = jnp.where(mask, a, 0.0)
        run = run + (suffix[:, 0:1] - nfail[:, 0:1])
        return run, acc + _dot(a.astype(BF16), v)

    k_diag = pl.multiple_of(t0, SB_TILE)
    state = tuple(tile(p, k_diag, jnp.zeros((2 * SB_TILE, 1), F32), jnp.zeros((2 * SB_TILE, LANES), F32), diag_mask)
                  for p in range(SB_PAIRS))

    def cond(c):
        i, state = c
        top = functools.reduce(jnp.maximum, [run for run, _ in state])
        return jnp.logical_and(i <= qi, jnp.max(top) > -SB_SKIP)

    def body(c):
        i, state = c
        k0 = pl.multiple_of((qi - i) * SB_TILE, SB_TILE)
        return i + 1, tuple(tile(p, k0, run, acc, None) for p, (run, acc) in enumerate(state))

    _, state = lax.while_loop(cond, body, (jnp.int32(1), state))
    for p, (_, acc) in enumerate(state):
        o_ref[0, :, p * LANES:(p + 1) * LANES] = jnp.where(
            lane < HEAD_DIM, acc[:SB_TILE], acc[SB_TILE:]).astype(o_ref.dtype)


def _sb_attention(qb, kb, vb, upper):
    b, s, w = qb.shape
    wb = SB_PAIRS * LANES
    return pl.pallas_call(
        _sb_kernel,
        grid=(b, w // wb, s // SB_TILE),
        in_specs=[pl.BlockSpec((1, SB_TILE, wb), lambda i, j, t: (i, t, j)),
                  pl.BlockSpec((1, s, wb), lambda i, j, t: (i, 0, j)),
                  pl.BlockSpec((1, s, wb), lambda i, j, t: (i, 0, j)),
                  _resident(upper)],
        out_specs=pl.BlockSpec((1, SB_TILE, wb), lambda i, j, t: (i, t, j)),
        out_shape=jax.ShapeDtypeStruct((b, s, w), BF16),
        compiler_params=pltpu.CompilerParams(
            dimension_semantics=("parallel", "parallel", "parallel"), vmem_limit_bytes=VMEM_LIMIT),
        name="sb_attention",
    )(qb, kb, vb, upper)


MERGE_TM = 256


def _merge_kernel(oa_ref, ob_ref, gm_ref, x_ref, wa_ref, wb_ref, wo_ref, g_ref, b_ref, y_ref, ybf_ref):
    ya = _dot(oa_ref[...], wa_ref[...])
    yb = _dot(ob_ref[...], wb_ref[...])
    merged = gm_ref[:, :D_MODEL] * ya + gm_ref[:, D_MODEL:] * yb
    y = _dot(merged.astype(BF16), wo_ref[...])
    out = _layer_norm(DEEPNORM_ALPHA * x_ref[...] + y, g_ref[...], b_ref[...])
    y_ref[...] = out
    ybf_ref[...] = out.astype(BF16)


def _merge(oa, ob, gm, x, wa, wb, wo, g, b):
    t = x.shape[0]
    tm = MERGE_TM
    row = lambda w: pl.BlockSpec((tm, w), lambda i: (i, 0))
    full = _resident
    return pl.pallas_call(
        _merge_kernel,
        grid=(t // tm,),
        in_specs=[row(NSA_Q), row(SB_W), row(2 * D_MODEL), row(D_MODEL),
                  full(wa), full(wb), full(wo), full(g), full(b)],
        out_specs=[row(D_MODEL), row(D_MODEL)],
        out_shape=[jax.ShapeDtypeStruct((t, D_MODEL), F32), jax.ShapeDtypeStruct((t, D_MODEL), BF16)],
        compiler_params=pltpu.CompilerParams(
            dimension_semantics=("parallel",), vmem_limit_bytes=VMEM_LIMIT),
        name="merge_out_ln",
    )(oa, ob, gm, x, wa, wb, wo, g, b)


FFN_TM = 512
FFN_CHUNK = D_FF // 2
HALO = 16


def _ffn_kernel(h_ref, halo_ref, x_ref, wu_ref, cw_ref, cb_ref, wd_ref, g_ref, b_ref, y_ref, ybf_ref,
                buf_ref, act_ref, *, tiles_per_seq):
    i = pl.program_id(0)
    has_prev = (i % tiles_per_seq) != 0
    tm = h_ref.shape[0]

    def conv_cols(c0):
        cols = slice(c0, c0 + FFN_CHUNK)
        buf_ref[0:HALO, :] = jnp.where(has_prev, _dot(halo_ref[...], wu_ref[:, cols]), 0.0)
        u = _dot(h_ref[...], wu_ref[:, cols])
        buf_ref[HALO:, :] = u
        return (cw_ref[0:1, cols] * buf_ref[HALO - 2:HALO - 2 + tm, :]
                + cw_ref[1:2, cols] * buf_ref[HALO - 1:HALO - 1 + tm, :]
                + cw_ref[2:3, cols] * u + cb_ref[:, cols])

    for c in range(D_FF // FFN_CHUNK):
        a = conv_cols(c * FFN_CHUNK)
        a = a * jax.nn.sigmoid(a)
        v = conv_cols(D_FF + c * FFN_CHUNK)
        act_ref[:, c * FFN_CHUNK:(c + 1) * FFN_CHUNK] = (a * v).astype(BF16)
    y = _dot(act_ref[...], wd_ref[...])
    out = _layer_norm(DEEPNORM_ALPHA * x_ref[...] + y, g_ref[...], b_ref[...])
    y_ref[...] = out
    ybf_ref[...] = out.astype(BF16)


def _conv_ffn(h_bf, x, w_up, conv_w, conv_b, w_down, g, b, seq):
    t = x.shape[0]
    tm = FFN_TM
    row = lambda w: pl.BlockSpec((tm, w), lambda i: (i, 0))
    return pl.pallas_call(
        functools.partial(_ffn_kernel, tiles_per_seq=seq // tm),
        grid=(t // tm,),
        in_specs=[row(D_MODEL),
                  pl.BlockSpec((HALO, D_MODEL), lambda i: (jnp.maximum(i * (tm // HALO) - 1, 0), 0)),
                  row(D_MODEL),
                  _resident(w_up), _resident(conv_w), _resident(conv_b), _resident(w_down),
                  _resident(g), _resident(b)],
        out_specs=[row(D_MODEL), row(D_MODEL)],
        out_shape=[jax.ShapeDtypeStruct((t, D_MODEL), F32), jax.ShapeDtypeStruct((t, D_MODEL), BF16)],
        scratch_shapes=[pltpu.VMEM((tm + HALO, FFN_CHUNK), F32), pltpu.VMEM((tm, D_FF), BF16)],
        compiler_params=pltpu.CompilerParams(
            dimension_semantics=("parallel",), vmem_limit_bytes=VMEM_LIMIT),
        name="conv_ffn_ln",
    )(h_bf, h_bf, x, w_up, conv_w, conv_b, w_down, g, b)


def _rope_tables(seq):
    inv_freq = ROPE_THETA ** (-np.arange(0, ROT_DIM, 2, dtype=np.float32) / ROT_DIM)
    ang = jnp.arange(seq, dtype=F32)[:, None] * jnp.asarray(inv_freq, F32)[None, :]
    cos, sin = jnp.cos(ang), jnp.sin(ang)
    half = ROT_DIM // 2
    ones = jnp.ones((seq, HEAD_DIM - ROT_DIM), F32)
    zeros = jnp.zeros((seq, HEAD_DIM - ROT_DIM), F32)
    zh = jnp.zeros((seq, half), F32)
    cos_t = jnp.concatenate([cos, cos, ones], axis=1)
    sin_prev = jnp.concatenate([zh, sin, zeros], axis=1)
    sin_next = jnp.concatenate([-sin, zh, zeros], axis=1)
    rep = LANES // HEAD_DIM
    return tuple(jnp.tile(a, (1, rep)) for a in (cos_t, sin_prev, sin_next))


_QA_PERM = np.concatenate([np.concatenate([np.arange(h * HEAD_DIM, (h + 1) * HEAD_DIM),
                                           np.arange((NSA_GROUP + h) * HEAD_DIM, (NSA_GROUP + h + 1) * HEAD_DIM)])
                           for h in range(NSA_GROUP)])


def _selection_overlap_t(seq):
    n_cmp = (seq - CMP_BLOCK) // CMP_STRIDE + 1
    n_sel = seq // SEL_BLOCK
    cs = np.arange(n_cmp) * CMP_STRIDE
    ce = cs + CMP_BLOCK
    ss = np.arange(n_sel) * SEL_BLOCK
    se = ss + SEL_BLOCK
    ov = np.clip(np.minimum(ce[:, None], se[None, :]) - np.maximum(cs[:, None], ss[None, :]), 0, None) / CMP_BLOCK
    out = np.zeros((n_sel, seq // CMP_STRIDE), np.float32)
    out[:, :n_cmp] = ov.T
    return jnp.asarray(out, BF16)


def _block_indicator(seq):
    e = np.zeros((seq, LANES), np.float32)
    e[np.arange(seq), np.arange(seq) // SEL_BLOCK] = 1.0
    return jnp.asarray(e, BF16)


def _gate_expand():
    x = np.zeros((LANES, 3 * NSA_Q), np.float32)
    for col in range(NSA_Q):
        hh = _QA_PERM[col] // HEAD_DIM
        for r in range(3):
            x[hh * 3 + r, r * NSA_Q + col] = 1.0
    return jnp.asarray(x, BF16)


def _neg_later_keys(n):
    u = -(np.arange(n)[:, None] > np.arange(n)[None, :]).astype(np.float32)
    return jnp.asarray(np.concatenate([u, u], axis=0), BF16)


def _prep_in_proj(w):
    q_a, kc, vc, ks, vs, kw, vw, g_a, q_b, k_b, v_b, g_m = jnp.split(w, SPLIT_POINTS, axis=-1)
    g_a = jnp.pad(g_a, ((0, 0), (0, LANES - g_a.shape[1])))
    cols = [q_a[:, _QA_PERM], kc, ks, kw, vc, vs, vw, q_b, k_b, v_b, g_a, g_m]
    return jnp.concatenate(cols, axis=1).astype(BF16)


def _prep_compress(pos, w1, b1, w2):
    half = CMP_BLOCK // 2
    pos2 = pos.reshape(2, half, 1, HEAD_DIM)
    pos_t = jnp.broadcast_to(pos2, (2, half, NSA_KV_HEADS, HEAD_DIM)).reshape(2, half * NSA_KV)
    w1r = w1.reshape(2, half, HEAD_DIM, CMP_HIDDEN)
    eye = jnp.eye(NSA_KV_HEADS, dtype=w1.dtype)
    w1x = (w1r[:, :, None, :, None, :] * eye[None, None, :, None, :, None]).reshape(
        2, half * NSA_KV, NSA_KV_HEADS * CMP_HIDDEN)
    b1t = jnp.tile(b1.reshape(1, CMP_HIDDEN), (1, NSA_KV_HEADS))
    w2x = (w2[None, :, None, :] * eye[:, None, :, None]).reshape(NSA_KV_HEADS * CMP_HIDDEN, NSA_KV)
    return pos_t, w1x[0].astype(BF16), w1x[1].astype(BF16), b1t, w2x.astype(BF16)


def kernel(x, w_in, cmp_pos_k, cmp_w1_k, cmp_b1_k, cmp_w2_k, cmp_pos_v, cmp_w1_v, cmp_b1_v, cmp_w2_v,
           w_branch_a, w_branch_b, w_out, ln_mix_g, ln_mix_b, w_up, conv_w, conv_b, w_down,
           ln_ffn_g, ln_ffn_b):
    b, s, d = x.shape
    t = b * s
    assert d == D_MODEL and s // SEL_BLOCK == N_SEL and s // CMP_STRIDE == LANES
    cos_t, sin_prev, sin_next = _rope_tables(s)
    ovt = _selection_overlap_t(s)
    expand = _block_indicator(s)
    gate_x = _gate_expand()
    upper = _neg_later_keys(SB_TILE)

    xf = x.reshape(t, d)
    xb = xf.astype(BF16)
    for l in range(DEPTH):
        outs = _in_proj(xb, _prep_in_proj(w_in[l]), cos_t, sin_prev, sin_next, s)
        qa, kcmp, ksel, kwin, vcmp, vsel, vwin, qb, kb, vb, ga, gm = outs
        rows = s // CMP_STRIDE
        kc = _compress(kcmp.reshape(b, rows, CMP_STRIDE * LANES),
                       *_prep_compress(cmp_pos_k[l], cmp_w1_k[l], cmp_b1_k[l], cmp_w2_k[l]))
        vc = _compress(vcmp.reshape(b, rows, CMP_STRIDE * LANES),
                       *_prep_compress(cmp_pos_v[l], cmp_w1_v[l], cmp_b1_v[l], cmp_w2_v[l]))
        r3 = lambda a: a.reshape(b, s, a.shape[-1])
        oa = _nsa_attention(r3(qa), kc, vc, r3(ksel), r3(vsel), r3(kwin), r3(vwin), r3(ga),
                            ovt, expand, gate_x)
        ob = _sb_attention(r3(qb), r3(kb), r3(vb), upper)
        xf, xb = _merge(oa.reshape(t, NSA_Q), ob.reshape(t, SB_W), gm, xf,
                        w_branch_a[l][_QA_PERM, :].astype(BF16), w_branch_b[l].astype(BF16),
                        w_out[l].astype(BF16), ln_mix_g[l].reshape(1, d), ln_mix_b[l].reshape(1, d))
        xf, xb = _conv_ffn(xb, xf, w_up[l].astype(BF16), conv_w[l], conv_b[l].reshape(1, -1),
                           w_down[l].astype(BF16), ln_ffn_g[l].reshape(1, d), ln_ffn_b[l].reshape(1, d), s)
    return xf.reshape(b, s, d)
```

```python
import functools

import numpy as np
import jax
import jax.numpy as jnp
from jax import lax
from jax.experimental import pallas as pl
from jax.experimental.pallas import tpu as pltpu

D_MODEL = 1024
DEPTH = 4
HEAD_DIM = 64
NSA_HEADS = 8
NSA_KV_HEADS = 2
NSA_GROUP = NSA_HEADS // NSA_KV_HEADS
SB_HEADS = 8
CMP_BLOCK = 32
CMP_STRIDE = 16
CMP_HIDDEN = 128
SEL_BLOCK = 64
SEL_TOPK = 16
WINDOW = 512
ROPE_THETA = 500000.0
ROT_DIM = HEAD_DIM // 4
D_FF = 2816
CONV_W = 3
LN_EPS = 1e-5
NEG = -1e30
FORCE = 1e4
DEEPNORM_ALPHA = (2.0 * DEPTH) ** 0.25

NSA_Q = NSA_HEADS * HEAD_DIM
NSA_KV = NSA_KV_HEADS * HEAD_DIM
SB_W = SB_HEADS * HEAD_DIM
SPLIT_SIZES = (NSA_Q, NSA_KV, NSA_KV, NSA_KV, NSA_KV, NSA_KV, NSA_KV, 3 * NSA_HEADS, SB_W, SB_W, SB_W, 2 * D_MODEL)
SPLIT_POINTS = tuple(int(v) for v in np.cumsum(SPLIT_SIZES)[:-1])

LANES = 128
VMEM_LIMIT = 56 * 1024 * 1024

BF16 = jnp.bfloat16
F32 = jnp.float32

Q_TILE = 128
SEL_KEY_TILE = 512
SB_TILE = 256
N_SEL = 32
SEL_CHAINS = 2


def _dot(a, b):
    return jnp.dot(a, b, preferred_element_type=F32)


def _dot_nt(a, b):
    return lax.dot_general(a, b, (((1,), (1,)), ((), ())), preferred_element_type=F32)


def _split_bf16(x):
    hi = x.astype(BF16)
    lo = (x - hi.astype(F32)).astype(BF16)
    return hi, lo


def _layer_norm(z, g, b):
    mu = jnp.mean(z, axis=-1, keepdims=True)
    zc = z - mu
    var = jnp.mean(zc * zc, axis=-1, keepdims=True)
    return zc * lax.rsqrt(var + LN_EPS) * g + b


def _resident(a):
    return pl.BlockSpec(a.shape, lambda *_: (0,) * a.ndim, pipeline_mode=pl.Buffered(1))


def _layer_block(a, l):
    return pl.BlockSpec((None,) + a.shape[1:], lambda *_: (l,) + (0,) * (a.ndim - 1), pipeline_mode=pl.Buffered(1))


_PROJ_OUT = (
    ("qa", NSA_Q, BF16, True, False, HEAD_DIM ** -0.5),
    ("kcmp", LANES, F32, True, False, 1.0),
    ("ksel", LANES, BF16, True, False, 1.0),
    ("kwin", LANES, BF16, True, False, 1.0),
    ("vcmp", LANES, F32, False, False, 1.0),
    ("vsel", LANES, BF16, False, False, 1.0),
    ("vwin", LANES, BF16, False, False, 1.0),
    ("qb", SB_W, BF16, False, False, HEAD_DIM ** -0.5),
    ("kb", SB_W, BF16, False, False, 1.0),
    ("vb", SB_W, BF16, False, False, 1.0),
    ("ga", LANES, F32, False, True, 1.0),
)
_PROJ_WIDTH = sum(o[1] for o in _PROJ_OUT)
PROJ_TM = 512


def _rope(acc, cos_t, sin_prev, sin_next):
    chunks = []
    for c in range(acc.shape[1] // LANES):
        xc = acc[:, c * LANES:(c + 1) * LANES]
        chunks.append(xc * cos_t + pltpu.roll(xc, ROT_DIM // 2, 1) * sin_prev
                      + pltpu.roll(xc, LANES - ROT_DIM // 2, 1) * sin_next)
    return chunks[0] if len(chunks) == 1 else jnp.concatenate(chunks, axis=1)


def _proj_kernel(x_ref, w_ref, cos_ref, sp_ref, sn_ref, *out_refs):
    x = x_ref[...]
    col = 0
    for (_, width, dtype, rotary, sigmoid, scale), o_ref in zip(_PROJ_OUT, out_refs):
        acc = _dot(x, w_ref[:, col:col + width])
        if rotary:
            acc = _rope(acc, cos_ref[...], sp_ref[...], sn_ref[...])
        if sigmoid:
            acc = jax.nn.sigmoid(acc)
        if scale != 1.0:
            acc = acc * scale
        o_ref[...] = acc.astype(dtype)
        col += width


def _in_proj(x_bf, w, layer, cos_t, sin_prev, sin_next, seq):
    t = x_bf.shape[0]
    tm = PROJ_TM
    tiles_per_seq = seq // tm
    tab_spec = pl.BlockSpec((tm, LANES), lambda i: (i % tiles_per_seq, 0))
    return pl.pallas_call(
        _proj_kernel,
        grid=(t // tm,),
        in_specs=[
            pl.BlockSpec((tm, D_MODEL), lambda i: (i, 0)),
            _layer_block(w, layer),
            tab_spec, tab_spec, tab_spec,
        ],
        out_specs=[pl.BlockSpec((tm, o[1]), lambda i: (i, 0)) for o in _PROJ_OUT],
        out_shape=[jax.ShapeDtypeStruct((t, o[1]), o[2]) for o in _PROJ_OUT],
        compiler_params=pltpu.CompilerParams(
            dimension_semantics=("parallel",), vmem_limit_bytes=VMEM_LIMIT),
        name="in_proj",
    )(x_bf, w, cos_t, sin_prev, sin_next)


def _gelu_tanh(x):
    return 0.5 * x * (1.0 + jnp.tanh(np.sqrt(2.0 / np.pi).astype(np.float32) * (x + 0.044715 * (x * x * x))))


def _compress_kernel(r_ref, pos_ref, w1a_ref, w1b_ref, b1_ref, w2_ref, o_ref):
    r = r_ref[0]
    ra = (r + pos_ref[0:1, :]).astype(BF16)
    rb = (r + pos_ref[1:2, :]).astype(BF16)
    p1 = _dot(ra, w1a_ref[...])
    p2 = _dot(rb, w1b_ref[...])
    nrow = p2.shape[0]
    hid = _gelu_tanh(p1 + pltpu.roll(p2, nrow - 1, 0) + b1_ref[...])
    o_ref[0] = _dot(hid.astype(BF16), w2_ref[...]).astype(o_ref.dtype)


def _compress(r, pos, w1a, w1b, b1, w2):
    b, nrow, width = r.shape
    full = _resident
    return pl.pallas_call(
        _compress_kernel,
        grid=(b,),
        in_specs=[pl.BlockSpec((1, nrow, width), lambda i: (i, 0, 0)),
                  full(pos), full(w1a), full(w1b), full(b1), full(w2)],
        out_specs=pl.BlockSpec((1, nrow, LANES), lambda i: (i, 0, 0)),
        out_shape=jax.ShapeDtypeStruct((b, nrow, LANES), BF16),
        compiler_params=pltpu.CompilerParams(
            dimension_semantics=("parallel",), vmem_limit_bytes=VMEM_LIMIT),
        name="compress",
    )(r, pos, w1a, w1b, b1, w2)


def _masked_softmax(s, mask):
    s = jnp.where(mask, s, NEG)
    m = jnp.max(s, axis=-1, keepdims=True)
    e = jnp.where(mask, jnp.exp(s - m), 0.0)
    return e / jnp.maximum(jnp.sum(e, axis=-1, keepdims=True), 1e-30)


def _tile_rows(a, n):
    return jnp.concatenate([a] * n, axis=0)


def _nsa_kernel(q_ref, kc_ref, vc_ref, ks_ref, vs_ref, kw_ref, vw_ref, ga_ref,
                ovt_ref, et_ref, x_ref, o_ref):
    qi = pl.program_id(1)
    t0 = qi * Q_TILE
    hq = NSA_GROUP * Q_TILE
    nrows = NSA_HEADS * Q_TILE
    lane = lax.broadcasted_iota(jnp.int32, (1, LANES), 1)
    t_col = t0 + lax.broadcasted_iota(jnp.int32, (Q_TILE, 1), 0)
    t_rows = _tile_rows(t_col, NSA_HEADS)
    q_all = q_ref[0]

    ga_hi, ga_lo = _split_bf16(ga_ref[0])
    gates = _dot(ga_hi, x_ref[...]) + _dot(ga_lo, x_ref[...])

    j_col = lax.broadcasted_iota(jnp.int32, (N_SEL, 1), 0)
    t_lane = t0 + lax.broadcasted_iota(jnp.int32, (1, Q_TILE), 1)
    cur = t_lane // SEL_BLOCK
    forced = (j_col == 0) | (j_col == cur) | (j_col == cur - 1)
    valid = j_col * SEL_BLOCK <= t_lane

    w0 = pl.multiple_of(jnp.maximum(t0 - WINDOW, 0), Q_TILE)
    wlen = WINDOW + Q_TILE
    diff = t_col - (w0 + lax.broadcasted_iota(jnp.int32, (1, wlen), 1))
    win_bias = _tile_rows(jnp.where((diff >= 0) & (diff < WINDOW), 0.0, NEG), NSA_HEADS)
    kw = kw_ref[0, pl.ds(w0, wlen), :]
    vw = vw_ref[0, pl.ds(w0, wlen), :]

    last = qi // (SEL_KEY_TILE // Q_TILE)
    k_last = pl.multiple_of(last * SEL_KEY_TILE, SEL_KEY_TILE)
    causal = _tile_rows((k_last + lax.broadcasted_iota(jnp.int32, (1, SEL_KEY_TILE), 1)) <= t_col, NSA_HEADS)

    qs = jnp.concatenate(
        [jnp.where((lane // HEAD_DIM) == g, q_all[:, h * LANES:(h + 1) * LANES], jnp.zeros((), BF16))
         for g in range(NSA_KV_HEADS) for h in range(NSA_GROUP)], axis=0)

    s = _dot_nt(qs, kc_ref[0])
    m_cmp = (CMP_STRIDE * lane + CMP_BLOCK - 1) <= t_rows
    p = _masked_softmax(s, m_cmp)
    o_cmp = _dot(p.astype(BF16), vc_ref[0])

    sw = _dot_nt(qs, kw) + win_bias
    ew = jnp.exp(sw - jnp.max(sw, axis=-1, keepdims=True)).astype(BF16)
    ow = _dot(ew, jnp.concatenate([vw, jnp.ones((wlen, LANES), BF16)], axis=1))
    o_win = ow[:, :LANES] / jnp.maximum(ow[:, LANES:], 1e-30)

    drops = []
    for g in range(NSA_KV_HEADS):
        pg = p[g * hq:(g + 1) * hq]
        psum = pg[0:Q_TILE] + pg[Q_TILE:2 * Q_TILE] + pg[2 * Q_TILE:3 * Q_TILE] + pg[3 * Q_TILE:4 * Q_TILE]
        ps_hi, ps_lo = _split_bf16(psum)
        score = _dot_nt(ovt_ref[...], ps_hi) + _dot_nt(ovt_ref[...], ps_lo)
        score = jnp.where(forced, FORCE, jnp.where(valid, score, -FORCE))
        rank = jnp.zeros((N_SEL, Q_TILE), F32)
        for jp in range(N_SEL):
            row = score[jp:jp + 1, :]
            ge = jnp.where(row >= score, 1.0, 0.0)
            gt = jnp.where(row > score, 1.0, 0.0)
            rank = rank + jnp.where(j_col > jp, ge, gt)
        drop_t = jnp.where(rank < SEL_TOPK, 0.0, NEG)
        drop_t = jnp.concatenate([drop_t, jnp.zeros((LANES - N_SEL, Q_TILE), F32)], axis=0)
        drop = drop_t.T.astype(BF16)
        drops.append(_tile_rows(drop, NSA_GROUP))
    q_aug = jnp.concatenate([qs, jnp.concatenate(drops, axis=0)], axis=1)

    crow = nrows // SEL_CHAINS
    q_parts = [q_aug[c * crow:(c + 1) * crow] for c in range(SEL_CHAINS)]

    def sel_tile(k0, carry, mask):
        k_aug = jnp.concatenate([ks_ref[0, pl.ds(k0, SEL_KEY_TILE), :],
                                 et_ref[pl.ds(k0, SEL_KEY_TILE), :]], axis=1)
        v_aug = jnp.concatenate([vs_ref[0, pl.ds(k0, SEL_KEY_TILE), :],
                                 jnp.ones((SEL_KEY_TILE, LANES), BF16)], axis=1)
        sc = [_dot_nt(q, k_aug) for q in q_parts]
        if mask is not None:
            sc = [jnp.where(mask[:crow], s_c, NEG) for s_c in sc]
        m_new = [jnp.maximum(m_i, jnp.max(s_c, axis=-1, keepdims=True)) for s_c, (m_i, _) in zip(sc, carry)]
        e = [jnp.exp(s_c - m_c).astype(BF16) for s_c, m_c in zip(sc, m_new)]
        return tuple((m_c, jnp.exp(m_i - m_c) * acc + _dot(e_c, v_aug))
                     for m_c, e_c, (m_i, acc) in zip(m_new, e, carry))

    carry = lax.fori_loop(
        0, last, lambda kt, c: sel_tile(pl.multiple_of(kt * SEL_KEY_TILE, SEL_KEY_TILE), c, None),
        ((jnp.full((crow, 1), NEG, F32), jnp.zeros((crow, 2 * LANES), F32)),) * SEL_CHAINS)
    acc = jnp.concatenate([a for _, a in sel_tile(k_last, carry, causal)], axis=0)
    o_sel = acc[:, :LANES] / jnp.maximum(acc[:, LANES:], 1e-30)

    branches = (o_cmp, o_sel, o_win)
    for h in range(NSA_GROUP):
        mixed = []
        for g in range(NSA_KV_HEADS):
            r0 = (g * NSA_GROUP + h) * Q_TILE
            acc = None
            for r in range(3):
                term = gates[:, r * NSA_Q + h * LANES: r * NSA_Q + (h + 1) * LANES] * branches[r][r0:r0 + Q_TILE]
                acc = term if acc is None else acc + term
            mixed.append(acc)
        o_ref[0, :, h * LANES:(h + 1) * LANES] = jnp.where(lane < HEAD_DIM, mixed[0], mixed[1]).astype(o_ref.dtype)


def _nsa_attention(qa, kc, vc, ksel, vsel, kwin, vwin, ga, ovt, expand, gate_x):
    b, s, _ = qa.shape
    per_q = lambda w: pl.BlockSpec((1, Q_TILE, w), lambda i, j: (i, j, 0))
    per_b = lambda a: pl.BlockSpec((1,) + a.shape[1:], lambda i, j: (i, 0, 0))
    full = _resident
    return pl.pallas_call(
        _nsa_kernel,
        grid=(b, s // Q_TILE),
        in_specs=[per_q(NSA_Q), per_b(kc), per_b(vc), per_b(ksel), per_b(vsel), per_b(kwin), per_b(vwin),
                  per_q(LANES), full(ovt), full(expand), full(gate_x)],
        out_specs=per_q(NSA_Q),
        out_shape=jax.ShapeDtypeStruct((b, s, NSA_Q), BF16),
        compiler_params=pltpu.CompilerParams(
            dimension_semantics=("parallel", "parallel"), vmem_limit_bytes=VMEM_LIMIT),
        name="nsa_attention",
    )(qa, kc, vc, ksel, vsel, kwin, vwin, ga, ovt, expand, gate_x)


SB_SKIP = 150.0
LOG2E = 1.4426950408889634
SB_PAIRS = 4


def _sb_kernel(q_ref, k_ref, v_ref, u_ref, o_ref):
    qi = pl.program_id(2)
    t0 = qi * SB_TILE
    lane = lax.broadcasted_iota(jnp.int32, (1, LANES), 1)
    t_col = t0 + lax.broadcasted_iota(jnp.int32, (SB_TILE, 1), 0)
    diag_mask = (t0 + lax.broadcasted_iota(jnp.int32, (1, SB_TILE), 1)) < t_col
    diag_mask = _tile_rows(diag_mask, 2)
    qms = []
    for p in range(SB_PAIRS):
        q = q_ref[0, :, p * LANES:(p + 1) * LANES]
        qms.append(jnp.concatenate(
            [jnp.where((lane // HEAD_DIM) == hl, q, jnp.zeros((), BF16)) for hl in range(2)], axis=0))

    def tiles(k0, state, mask):
        pairs = range(SB_PAIRS)
        cols = [slice(p * LANES, (p + 1) * LANES) for p in pairs]
        z2 = [_dot_nt(qms[p], k_ref[0, pl.ds(k0, SB_TILE), cols[p]]) * LOG2E for p in pairs]
        split = []
        for p in pairs:
            neg_abs = pltpu.bitcast(pltpu.bitcast(z2[p], jnp.uint32) | jnp.uint32(0x80000000), F32)
            nfail = jnp.maximum(z2[p], 0.0) + jnp.log2(1.0 + jnp.exp2(neg_abs))
            if mask is not None:
                nfail = jnp.where(mask, nfail, 0.0)
            split.append(jnp.concatenate(_split_bf16(nfail), axis=1))
        tail = [_dot(split[p], u_ref[...]) for p in pairs]
        a = []
        for p in pairs:
            ap = jnp.exp2(z2[p] + tail[p] + state[p][0])
            if mask is not None:
                ap = jnp.where(mask, ap, 0.0)
            a.append(ap.astype(BF16))
        return tuple((state[p][0] + tail[p][:, 0:1],
                      state[p][1] + _dot(a[p], v_ref[0, pl.ds(k0, SB_TILE), cols[p]])) for p in pairs)

    zero = (jnp.zeros((2 * SB_TILE, 1), F32), jnp.zeros((2 * SB_TILE, LANES), F32))
    state = tiles(pl.multiple_of(t0, SB_TILE), (zero,) * SB_PAIRS, diag_mask)

    def cond(c):
        i, state = c
        top = functools.reduce(jnp.maximum, [run for run, _ in state])
        return jnp.logical_and(i <= qi, jnp.max(top) > -SB_SKIP)

    def body(c):
        i, state = c
        k0 = pl.multiple_of((qi - i) * SB_TILE, SB_TILE)
        return i + 1, tiles(k0, state, None)

    _, state = lax.while_loop(cond, body, (jnp.int32(1), state))
    for p, (_, acc) in enumerate(state):
        o_ref[0, :, p * LANES:(p + 1) * LANES] = jnp.where(
            lane < HEAD_DIM, acc[:SB_TILE], acc[SB_TILE:]).astype(o_ref.dtype)


def _sb_attention(qb, kb, vb, upper):
    b, s, w = qb.shape
    wb = SB_PAIRS * LANES
    return pl.pallas_call(
        _sb_kernel,
        grid=(b, w // wb, s // SB_TILE),
        in_specs=[pl.BlockSpec((1, SB_TILE, wb), lambda i, j, t: (i, t, j)),
                  pl.BlockSpec((1, s, wb), lambda i, j, t: (i, 0, j)),
                  pl.BlockSpec((1, s, wb), lambda i, j, t: (i, 0, j)),
                  _resident(upper)],
        out_specs=pl.BlockSpec((1, SB_TILE, wb), lambda i, j, t: (i, t, j)),
        out_shape=jax.ShapeDtypeStruct((b, s, w), BF16),
        compiler_params=pltpu.CompilerParams(
            dimension_semantics=("parallel", "parallel", "parallel"), vmem_limit_bytes=VMEM_LIMIT),
        name="sb_attention",
    )(qb, kb, vb, upper)


MERGE_TM = 256


def _merge_kernel(oa_ref, ob_ref, xb_ref, x_ref, wg_ref, wa_ref, wb_ref, wo_ref, g_ref, b_ref, y_ref, ybf_ref):
    xb = xb_ref[...]
    ya = _dot(oa_ref[...], wa_ref[...])
    merged = jax.nn.sigmoid(_dot(xb, wg_ref[:, :D_MODEL])) * ya
    yb = _dot(ob_ref[...], wb_ref[...])
    merged = merged + jax.nn.sigmoid(_dot(xb, wg_ref[:, D_MODEL:])) * yb
    y = _dot(merged.astype(BF16), wo_ref[...])
    out = _layer_norm(DEEPNORM_ALPHA * x_ref[...] + y, g_ref[...], b_ref[...])
    y_ref[...] = out
    ybf_ref[...] = out.astype(BF16)


def _merge(oa, ob, x_bf, x, layer, wg, wa, wb, wo, g, b):
    t = x.shape[0]
    tm = MERGE_TM
    row = lambda w: pl.BlockSpec((tm, w), lambda i: (i, 0))
    full = lambda a: _layer_block(a, layer)
    return pl.pallas_call(
        _merge_kernel,
        grid=(t // tm,),
        in_specs=[row(NSA_Q), row(SB_W), row(D_MODEL), row(D_MODEL),
                  full(wg), full(wa), full(wb), full(wo), full(g), full(b)],
        out_specs=[row(D_MODEL), row(D_MODEL)],
        out_shape=[jax.ShapeDtypeStruct((t, D_MODEL), F32), jax.ShapeDtypeStruct((t, D_MODEL), BF16)],
        compiler_params=pltpu.CompilerParams(
            dimension_semantics=("parallel",), vmem_limit_bytes=VMEM_LIMIT),
        name="merge_out_ln",
    )(oa, ob, x_bf, x, wg, wa, wb, wo, g, b)


FFN_TM = 512
FFN_CHUNK = D_FF // 2
HALO = 16


def _ffn_kernel(h_ref, halo_ref, x_ref, wu_ref, cw_ref, cb_ref, wd_ref, g_ref, b_ref, y_ref, ybf_ref,
                buf_ref, act_ref, *, tiles_per_seq):
    i = pl.program_id(0)
    has_prev = (i % tiles_per_seq) != 0
    tm = h_ref.shape[0]

    def conv_cols(c0):
        cols = slice(c0, c0 + FFN_CHUNK)
        buf_ref[0:HALO, :] = jnp.where(has_prev, _dot(halo_ref[...], wu_ref[:, cols]), 0.0)
        u = _dot(h_ref[...], wu_ref[:, cols])
        buf_ref[HALO:, :] = u
        return (cw_ref[0:1, cols] * buf_ref[HALO - 2:HALO - 2 + tm, :]
                + cw_ref[1:2, cols] * buf_ref[HALO - 1:HALO - 1 + tm, :]
                + cw_ref[2:3, cols] * u + cb_ref[:, cols])

    for c in range(D_FF // FFN_CHUNK):
        a = conv_cols(c * FFN_CHUNK)
        a = a * jax.nn.sigmoid(a)
        v = conv_cols(D_FF + c * FFN_CHUNK)
        act_ref[:, c * FFN_CHUNK:(c + 1) * FFN_CHUNK] = (a * v).astype(BF16)
    y = _dot(act_ref[...], wd_ref[...])
    out = _layer_norm(DEEPNORM_ALPHA * x_ref[...] + y, g_ref[...], b_ref[...])
    y_ref[...] = out
    ybf_ref[...] = out.astype(BF16)


def _conv_ffn(h_bf, x, layer, w_up, conv_w, conv_b, w_down, g, b, seq):
    t = x.shape[0]
    tm = FFN_TM
    row = lambda w: pl.BlockSpec((tm, w), lambda i: (i, 0))
    return pl.pallas_call(
        functools.partial(_ffn_kernel, tiles_per_seq=seq // tm),
        grid=(t // tm,),
        in_specs=[row(D_MODEL),
                  pl.BlockSpec((HALO, D_MODEL), lambda i: (jnp.maximum(i * (tm // HALO) - 1, 0), 0)),
                  row(D_MODEL),
                  *[_layer_block(a, layer) for a in (w_up, conv_w, conv_b, w_down, g, b)]],
        out_specs=[row(D_MODEL), row(D_MODEL)],
        out_shape=[jax.ShapeDtypeStruct((t, D_MODEL), F32), jax.ShapeDtypeStruct((t, D_MODEL), BF16)],
        scratch_shapes=[pltpu.VMEM((tm + HALO, FFN_CHUNK), F32), pltpu.VMEM((tm, D_FF), BF16)],
        compiler_params=pltpu.CompilerParams(
            dimension_semantics=("parallel",), vmem_limit_bytes=VMEM_LIMIT),
        name="conv_ffn_ln",
    )(h_bf, h_bf, x, w_up, conv_w, conv_b, w_down, g, b)


def _rope_tables(seq):
    inv_freq = ROPE_THETA ** (-np.arange(0, ROT_DIM, 2, dtype=np.float32) / ROT_DIM)
    ang = jnp.arange(seq, dtype=F32)[:, None] * jnp.asarray(inv_freq, F32)[None, :]
    cos, sin = jnp.cos(ang), jnp.sin(ang)
    half = ROT_DIM // 2
    ones = jnp.ones((seq, HEAD_DIM - ROT_DIM), F32)
    zeros = jnp.zeros((seq, HEAD_DIM - ROT_DIM), F32)
    zh = jnp.zeros((seq, half), F32)
    cos_t = jnp.concatenate([cos, cos, ones], axis=1)
    sin_prev = jnp.concatenate([zh, sin, zeros], axis=1)
    sin_next = jnp.concatenate([-sin, zh, zeros], axis=1)
    rep = LANES // HEAD_DIM
    return tuple(jnp.tile(a, (1, rep)) for a in (cos_t, sin_prev, sin_next))


_QA_PERM = np.concatenate([np.concatenate([np.arange(h * HEAD_DIM, (h + 1) * HEAD_DIM),
                                           np.arange((NSA_GROUP + h) * HEAD_DIM, (NSA_GROUP + h + 1) * HEAD_DIM)])
                           for h in range(NSA_GROUP)])


def _selection_overlap_t(seq):
    n_cmp = (seq - CMP_BLOCK) // CMP_STRIDE + 1
    n_sel = seq // SEL_BLOCK
    cs = np.arange(n_cmp) * CMP_STRIDE
    ce = cs + CMP_BLOCK
    ss = np.arange(n_sel) * SEL_BLOCK
    se = ss + SEL_BLOCK
    ov = np.clip(np.minimum(ce[:, None], se[None, :]) - np.maximum(cs[:, None], ss[None, :]), 0, None) / CMP_BLOCK
    out = np.zeros((n_sel, seq // CMP_STRIDE), np.float32)
    out[:, :n_cmp] = ov.T
    return jnp.asarray(out, BF16)


def _block_indicator(seq):
    e = np.zeros((seq, LANES), np.float32)
    e[np.arange(seq), np.arange(seq) // SEL_BLOCK] = 1.0
    return jnp.asarray(e, BF16)


def _gate_expand():
    x = np.zeros((LANES, 3 * NSA_Q), np.float32)
    for col in range(NSA_Q):
        hh = _QA_PERM[col] // HEAD_DIM
        for r in range(3):
            x[hh * 3 + r, r * NSA_Q + col] = 1.0
    return jnp.asarray(x, BF16)


def _neg_later_keys(n):
    u = -(np.arange(n)[:, None] >= np.arange(n)[None, :]).astype(np.float32)
    return jnp.asarray(np.concatenate([u, u], axis=0), BF16)


def _prep_in_proj(w):
    q_a, kc, vc, ks, vs, kw, vw, g_a, q_b, k_b, v_b, g_m = jnp.split(w.astype(BF16), SPLIT_POINTS, axis=-1)
    g_a = jnp.pad(g_a, ((0, 0), (0, 0), (0, LANES - g_a.shape[-1])))
    cols = [q_a[..., _QA_PERM], kc, ks, kw, vc, vs, vw, q_b, k_b, v_b, g_a]
    return jnp.concatenate(cols, axis=-1), g_m


def _prep_compress(pos, w1, b1, w2):
    half = CMP_BLOCK // 2
    pos2 = pos.reshape(2, half, 1, HEAD_DIM)
    pos_t = jnp.broadcast_to(pos2, (2, half, NSA_KV_HEADS, HEAD_DIM)).reshape(2, half * NSA_KV)
    w1r = w1.reshape(2, half, HEAD_DIM, CMP_HIDDEN)
    eye = jnp.eye(NSA_KV_HEADS, dtype=w1.dtype)
    w1x = (w1r[:, :, None, :, None, :] * eye[None, None, :, None, :, None]).reshape(
        2, half * NSA_KV, NSA_KV_HEADS * CMP_HIDDEN)
    b1t = jnp.tile(b1.reshape(1, CMP_HIDDEN), (1, NSA_KV_HEADS))
    w2x = (w2[None, :, None, :] * eye[:, None, :, None]).reshape(NSA_KV_HEADS * CMP_HIDDEN, NSA_KV)
    return pos_t, w1x[0].astype(BF16), w1x[1].astype(BF16), b1t, w2x.astype(BF16)


def kernel(x, w_in, cmp_pos_k, cmp_w1_k, cmp_b1_k, cmp_w2_k, cmp_pos_v, cmp_w1_v, cmp_b1_v, cmp_w2_v,
           w_branch_a, w_branch_b, w_out, ln_mix_g, ln_mix_b, w_up, conv_w, conv_b, w_down,
           ln_ffn_g, ln_ffn_b):
    b, s, d = x.shape
    t = b * s
    assert d == D_MODEL and s // SEL_BLOCK == N_SEL and s // CMP_STRIDE == LANES
    cos_t, sin_prev, sin_next = _rope_tables(s)
    ovt = _selection_overlap_t(s)
    expand = _block_indicator(s)
    gate_x = _gate_expand()
    upper = _neg_later_keys(SB_TILE)

    w_proj, w_gate = _prep_in_proj(w_in)
    wa = w_branch_a[:, _QA_PERM, :].astype(BF16)
    wb = w_branch_b.astype(BF16)
    wo = w_out.astype(BF16)
    wu = w_up.astype(BF16)
    wd = w_down.astype(BF16)
    stack_row = lambda a: a.reshape(a.shape[0], 1, a.shape[1])
    cb, g_mix, b_mix, g_ffn, b_ffn = map(stack_row, (conv_b, ln_mix_g, ln_mix_b, ln_ffn_g, ln_ffn_b))

    xf = x.reshape(t, d)
    xb = xf.astype(BF16)
    for l in range(DEPTH):
        outs = _in_proj(xb, w_proj, l, cos_t, sin_prev, sin_next, s)
        qa, kcmp, ksel, kwin, vcmp, vsel, vwin, qb, kb, vb, ga = outs
        rows = s // CMP_STRIDE
        kc = _compress(kcmp.reshape(b, rows, CMP_STRIDE * LANES),
                       *_prep_compress(cmp_pos_k[l], cmp_w1_k[l], cmp_b1_k[l], cmp_w2_k[l]))
        vc = _compress(vcmp.reshape(b, rows, CMP_STRIDE * LANES),
                       *_prep_compress(cmp_pos_v[l], cmp_w1_v[l], cmp_b1_v[l], cmp_w2_v[l]))
        r3 = lambda a: a.reshape(b, s, a.shape[-1])
        oa = _nsa_attention(r3(qa), kc, vc, r3(ksel), r3(vsel), r3(kwin), r3(vwin), r3(ga),
                            ovt, expand, gate_x)
        ob = _sb_attention(r3(qb), r3(kb), r3(vb), upper)
        xf, xb = _merge(oa.reshape(t, NSA_Q), ob.reshape(t, SB_W), xb, xf, l,
                        w_gate, wa, wb, wo, g_mix, b_mix)
        xf, xb = _conv_ffn(xb, xf, l, wu, conv_w, cb, wd, g_ffn, b_ffn, s)
    return xf.reshape(b, s, d)
```

```python
import functools

import numpy as np
import jax
import jax.numpy as jnp
from jax import lax
from jax.experimental import pallas as pl
from jax.experimental.pallas import tpu as pltpu

D_MODEL = 1024
DEPTH = 4
HEAD_DIM = 64
NSA_HEADS = 8
NSA_KV_HEADS = 2
NSA_GROUP = NSA_HEADS // NSA_KV_HEADS
SB_HEADS = 8
CMP_BLOCK = 32
CMP_STRIDE = 16
CMP_HIDDEN = 128
SEL_BLOCK = 64
SEL_TOPK = 16
WINDOW = 512
ROPE_THETA = 500000.0
ROT_DIM = HEAD_DIM // 4
D_FF = 2816
CONV_W = 3
LN_EPS = 1e-5
NEG = -1e30
FORCE = 1e4
DEEPNORM_ALPHA = (2.0 * DEPTH) ** 0.25

NSA_Q = NSA_HEADS * HEAD_DIM
NSA_KV = NSA_KV_HEADS * HEAD_DIM
SB_W = SB_HEADS * HEAD_DIM
SPLIT_SIZES = (NSA_Q, NSA_KV, NSA_KV, NSA_KV, NSA_KV, NSA_KV, NSA_KV, 3 * NSA_HEADS, SB_W, SB_W, SB_W, 2 * D_MODEL)
SPLIT_POINTS = tuple(int(v) for v in np.cumsum(SPLIT_SIZES)[:-1])

LANES = 128
VMEM_LIMIT = 56 * 1024 * 1024

BF16 = jnp.bfloat16
F32 = jnp.float32

Q_TILE = 128
SEL_KEY_TILE = 512
SB_TILE = 256
N_SEL = 32
SEL_CHAINS = 2


def _dot(a, b):
    return jnp.dot(a, b, preferred_element_type=F32)


def _dot_nt(a, b):
    return lax.dot_general(a, b, (((1,), (1,)), ((), ())), preferred_element_type=F32)


def _split_bf16(x):
    hi = x.astype(BF16)
    lo = (x - hi.astype(F32)).astype(BF16)
    return hi, lo


def _layer_norm(z, g, b):
    mu = jnp.mean(z, axis=-1, keepdims=True)
    zc = z - mu
    var = jnp.mean(zc * zc, axis=-1, keepdims=True)
    return zc * lax.rsqrt(var + LN_EPS) * g + b


def _resident(a):
    return pl.BlockSpec(a.shape, lambda *_: (0,) * a.ndim, pipeline_mode=pl.Buffered(1))


def _layer_block(a, l):
    return pl.BlockSpec((None,) + a.shape[1:], lambda *_: (l,) + (0,) * (a.ndim - 1), pipeline_mode=pl.Buffered(1))


_PROJ_OUT = (
    ("qa", NSA_Q, BF16, True, False, HEAD_DIM ** -0.5),
    ("kcmp", LANES, F32, True, False, 1.0),
    ("ksel", LANES, BF16, True, False, 1.0),
    ("kwin", LANES, BF16, True, False, 1.0),
    ("vcmp", LANES, F32, False, False, 1.0),
    ("vsel", LANES, BF16, False, False, 1.0),
    ("vwin", LANES, BF16, False, False, 1.0),
    ("qb", SB_W, BF16, False, False, HEAD_DIM ** -0.5),
    ("kb", SB_W, BF16, False, False, 1.0),
    ("vb", SB_W, BF16, False, False, 1.0),
    ("ga", LANES, F32, False, True, 1.0),
)
_PROJ_WIDTH = sum(o[1] for o in _PROJ_OUT)
_ROWS16 = ("kcmp", "vcmp")
_GATE_WIDTH = 2 * D_MODEL
PROJ_TM = 512


def _rope(acc, cos_t, sin_prev, sin_next):
    chunks = []
    for c in range(acc.shape[1] // LANES):
        xc = acc[:, c * LANES:(c + 1) * LANES]
        chunks.append(xc * cos_t + pltpu.roll(xc, ROT_DIM // 2, 1) * sin_prev
                      + pltpu.roll(xc, LANES - ROT_DIM // 2, 1) * sin_next)
    return chunks[0] if len(chunks) == 1 else jnp.concatenate(chunks, axis=1)


def _proj_kernel(x_ref, w_ref, cos_ref, sp_ref, sn_ref, *refs):
    out_refs, stage_ref = refs[:-1], refs[-1]
    x = x_ref[...]
    tm = x.shape[0]

    def finish(acc, spec, o_ref):
        name, _, dtype, rotary, sigmoid, scale = spec
        if rotary:
            acc = _rope(acc, cos_ref[...], sp_ref[...], sn_ref[...])
        if sigmoid:
            acc = jax.nn.sigmoid(acc)
        if scale != 1.0:
            acc = acc * scale
        if name in _ROWS16:
            stage_ref[...] = acc
            for l in range(CMP_STRIDE):
                o_ref[:, l * LANES:(l + 1) * LANES] = stage_ref[pl.ds(l, tm // CMP_STRIDE, stride=CMP_STRIDE), :]
        else:
            o_ref[...] = acc.astype(dtype)

    col = _GATE_WIDTH
    pending = None
    for spec, o_ref in zip(_PROJ_OUT, out_refs):
        acc = _dot(x, w_ref[:, col:col + spec[1]])
        if pending is not None:
            finish(*pending)
        pending = (acc, spec, o_ref)
        col += spec[1]
    finish(*pending)


def _in_proj(x_bf, w, layer, cos_t, sin_prev, sin_next, seq):
    t = x_bf.shape[0]
    tm = PROJ_TM
    tiles_per_seq = seq // tm
    tab_spec = pl.BlockSpec((tm, LANES), lambda i: (i % tiles_per_seq, 0))
    fold = lambda o: CMP_STRIDE if o[0] in _ROWS16 else 1
    return pl.pallas_call(
        _proj_kernel,
        grid=(t // tm,),
        in_specs=[
            pl.BlockSpec((tm, D_MODEL), lambda i: (i, 0)),
            _layer_block(w, layer),
            tab_spec, tab_spec, tab_spec,
        ],
        out_specs=[pl.BlockSpec((tm // fold(o), o[1] * fold(o)), lambda i: (i, 0)) for o in _PROJ_OUT],
        out_shape=[jax.ShapeDtypeStruct((t // fold(o), o[1] * fold(o)), o[2]) for o in _PROJ_OUT],
        scratch_shapes=[pltpu.VMEM((tm, LANES), F32)],
        compiler_params=pltpu.CompilerParams(
            dimension_semantics=("parallel",), vmem_limit_bytes=VMEM_LIMIT),
        name="in_proj",
    )(x_bf, w, cos_t, sin_prev, sin_next)


def _gelu_tanh(x):
    return 0.5 * x * (1.0 + jnp.tanh(np.sqrt(2.0 / np.pi).astype(np.float32) * (x + 0.044715 * (x * x * x))))


def _compress_kernel(r_ref, pos_ref, w1a_ref, w1b_ref, b1_ref, w2_ref, o_ref):
    r = r_ref[0]
    ra = (r + pos_ref[0:1, :]).astype(BF16)
    rb = (r + pos_ref[1:2, :]).astype(BF16)
    p1 = _dot(ra, w1a_ref[...])
    p2 = _dot(rb, w1b_ref[...])
    nrow = p2.shape[0]
    hid = _gelu_tanh(p1 + pltpu.roll(p2, nrow - 1, 0) + b1_ref[...])
    o_ref[0] = _dot(hid.astype(BF16), w2_ref[...]).astype(o_ref.dtype)


def _compress(r, pos, w1a, w1b, b1, w2):
    b, nrow, width = r.shape
    full = _resident
    return pl.pallas_call(
        _compress_kernel,
        grid=(b,),
        in_specs=[pl.BlockSpec((1, nrow, width), lambda i: (i, 0, 0)),
                  full(pos), full(w1a), full(w1b), full(b1), full(w2)],
        out_specs=pl.BlockSpec((1, nrow, LANES), lambda i: (i, 0, 0)),
        out_shape=jax.ShapeDtypeStruct((b, nrow, LANES), BF16),
        compiler_params=pltpu.CompilerParams(
            dimension_semantics=("parallel",), vmem_limit_bytes=VMEM_LIMIT),
        name="compress",
    )(r, pos, w1a, w1b, b1, w2)


def _masked_softmax(s, mask):
    s = jnp.where(mask, s, NEG)
    m = jnp.max(s, axis=-1, keepdims=True)
    e = jnp.where(mask, jnp.exp(s - m), 0.0)
    return e / jnp.maximum(jnp.sum(e, axis=-1, keepdims=True), 1e-30)


def _tile_rows(a, n):
    return jnp.concatenate([a] * n, axis=0)


def _nsa_kernel(q_ref, kc_ref, vc_ref, ks_ref, vs_ref, kw_ref, vw_ref, ga_ref,
                ovt_ref, et_ref, x_ref, o_ref):
    qi = pl.program_id(1)
    t0 = qi * Q_TILE
    hq = NSA_GROUP * Q_TILE
    nrows = NSA_HEADS * Q_TILE
    lane = lax.broadcasted_iota(jnp.int32, (1, LANES), 1)
    t_col = t0 + lax.broadcasted_iota(jnp.int32, (Q_TILE, 1), 0)
    t_rows = _tile_rows(t_col, NSA_HEADS)
    q_all = q_ref[0]

    ga_hi, ga_lo = _split_bf16(ga_ref[0])
    gates = _dot(ga_hi, x_ref[...]) + _dot(ga_lo, x_ref[...])

    j_col = lax.broadcasted_iota(jnp.int32, (N_SEL, 1), 0)
    t_lane = t0 + lax.broadcasted_iota(jnp.int32, (1, Q_TILE), 1)
    cur = t_lane // SEL_BLOCK
    forced = (j_col == 0) | (j_col == cur) | (j_col == cur - 1)
    valid = j_col * SEL_BLOCK <= t_lane

    w0 = pl.multiple_of(jnp.maximum(t0 - WINDOW, 0), Q_TILE)
    wlen = WINDOW + Q_TILE
    diff = t_col - (w0 + lax.broadcasted_iota(jnp.int32, (1, wlen), 1))
    win_bias = _tile_rows(jnp.where((diff >= 0) & (diff < WINDOW), 0.0, NEG), NSA_HEADS)
    kw = kw_ref[0, pl.ds(w0, wlen), :]
    vw = vw_ref[0, pl.ds(w0, wlen), :]

    last = qi // (SEL_KEY_TILE // Q_TILE)
    k_last = pl.multiple_of(last * SEL_KEY_TILE, SEL_KEY_TILE)
    causal = _tile_rows((k_last + lax.broadcasted_iota(jnp.int32, (1, SEL_KEY_TILE), 1)) <= t_col, NSA_HEADS)

    qs = jnp.concatenate(
        [jnp.where((lane // HEAD_DIM) == g, q_all[:, h * LANES:(h + 1) * LANES], jnp.zeros((), BF16))
         for g in range(NSA_KV_HEADS) for h in range(NSA_GROUP)], axis=0)

    s = _dot_nt(qs, kc_ref[0])
    m_cmp = (CMP_STRIDE * lane + CMP_BLOCK - 1) <= t_rows
    p = _masked_softmax(s, m_cmp)
    o_cmp = _dot(p.astype(BF16), vc_ref[0])

    sw = _dot_nt(qs, kw) + win_bias
    ew = jnp.exp(sw - jnp.max(sw, axis=-1, keepdims=True)).astype(BF16)
    ow = _dot(ew, jnp.concatenate([vw, jnp.ones((wlen, LANES), BF16)], axis=1))
    o_win = ow[:, :LANES] / jnp.maximum(ow[:, LANES:], 1e-30)

    drops = []
    for g in range(NSA_KV_HEADS):
        pg = p[g * hq:(g + 1) * hq]
        psum = pg[0:Q_TILE] + pg[Q_TILE:2 * Q_TILE] + pg[2 * Q_TILE:3 * Q_TILE] + pg[3 * Q_TILE:4 * Q_TILE]
        ps_hi, ps_lo = _split_bf16(psum)
        score = _dot_nt(ovt_ref[...], ps_hi) + _dot_nt(ovt_ref[...], ps_lo)
        score = jnp.where(forced, FORCE, jnp.where(valid, score, -FORCE))
        rank = jnp.zeros((N_SEL, Q_TILE), F32)
        for jp in range(N_SEL):
            row = score[jp:jp + 1, :]
            ge = jnp.where(row >= score, 1.0, 0.0)
            gt = jnp.where(row > score, 1.0, 0.0)
            rank = rank + jnp.where(j_col > jp, ge, gt)
        drop_t = jnp.where(rank < SEL_TOPK, 0.0, NEG)
        drop_t = jnp.concatenate([drop_t, jnp.zeros((LANES - N_SEL, Q_TILE), F32)], axis=0)
        drop = drop_t.T.astype(BF16)
        drops.append(_tile_rows(drop, NSA_GROUP))
    q_aug = jnp.concatenate([qs, jnp.concatenate(drops, axis=0)], axis=1)

    crow = nrows // SEL_CHAINS
    q_parts = [q_aug[c * crow:(c + 1) * crow] for c in range(SEL_CHAINS)]

    def sel_tile(k0, carry, mask):
        k_aug = jnp.concatenate([ks_ref[0, pl.ds(k0, SEL_KEY_TILE), :],
                                 et_ref[pl.ds(k0, SEL_KEY_TILE), :]], axis=1)
        v_aug = jnp.concatenate([vs_ref[0, pl.ds(k0, SEL_KEY_TILE), :],
                                 jnp.ones((SEL_KEY_TILE, LANES), BF16)], axis=1)
        sc = [_dot_nt(q, k_aug) for q in q_parts]
        if mask is not None:
            sc = [jnp.where(mask[:crow], s_c, NEG) for s_c in sc]
        m_new = [jnp.maximum(m_i, jnp.max(s_c, axis=-1, keepdims=True)) for s_c, (m_i, _) in zip(sc, carry)]
        e = [jnp.exp(s_c - m_c).astype(BF16) for s_c, m_c in zip(sc, m_new)]
        return tuple((m_c, jnp.exp(m_i - m_c) * acc + _dot(e_c, v_aug))
                     for m_c, e_c, (m_i, acc) in zip(m_new, e, carry))

    carry = lax.fori_loop(
        0, last, lambda kt, c: sel_tile(pl.multiple_of(kt * SEL_KEY_TILE, SEL_KEY_TILE), c, None),
        ((jnp.full((crow, 1), NEG, F32), jnp.zeros((crow, 2 * LANES), F32)),) * SEL_CHAINS)
    acc = jnp.concatenate([a for _, a in sel_tile(k_last, carry, causal)], axis=0)
    o_sel = acc[:, :LANES] / jnp.maximum(acc[:, LANES:], 1e-30)

    branches = (o_cmp, o_sel, o_win)
    for h in range(NSA_GROUP):
        mixed = []
        for g in range(NSA_KV_HEADS):
            r0 = (g * NSA_GROUP + h) * Q_TILE
            acc = None
            for r in range(3):
                term = gates[:, r * NSA_Q + h * LANES: r * NSA_Q + (h + 1) * LANES] * branches[r][r0:r0 + Q_TILE]
                acc = term if acc is None else acc + term
            mixed.append(acc)
        o_ref[0, :, h * LANES:(h + 1) * LANES] = jnp.where(lane < HEAD_DIM, mixed[0], mixed[1]).astype(o_ref.dtype)


def _nsa_attention(qa, kc, vc, ksel, vsel, kwin, vwin, ga, ovt, expand, gate_x):
    b, s, _ = qa.shape
    per_q = lambda w: pl.BlockSpec((1, Q_TILE, w), lambda i, j: (i, j, 0))
    per_b = lambda a: pl.BlockSpec((1,) + a.shape[1:], lambda i, j: (i, 0, 0))
    full = _resident
    return pl.pallas_call(
        _nsa_kernel,
        grid=(b, s // Q_TILE),
        in_specs=[per_q(NSA_Q), per_b(kc), per_b(vc), per_b(ksel), per_b(vsel), per_b(kwin), per_b(vwin),
                  per_q(LANES), full(ovt), full(expand), full(gate_x)],
        out_specs=per_q(NSA_Q),
        out_shape=jax.ShapeDtypeStruct((b, s, NSA_Q), BF16),
        compiler_params=pltpu.CompilerParams(
            dimension_semantics=("parallel", "parallel"), vmem_limit_bytes=VMEM_LIMIT),
        name="nsa_attention",
    )(qa, kc, vc, ksel, vsel, kwin, vwin, ga, ovt, expand, gate_x)


SB_SKIP = 150.0
LOG2E = 1.4426950408889634
SB_PAIRS = 4


def _sb_kernel(q_ref, k_ref, v_ref, u_ref, o_ref):
    qi = pl.program_id(2)
    t0 = qi * SB_TILE
    lane = lax.broadcasted_iota(jnp.int32, (1, LANES), 1)
    t_col = t0 + lax.broadcasted_iota(jnp.int32, (SB_TILE, 1), 0)
    diag_mask = (t0 + lax.broadcasted_iota(jnp.int32, (1, SB_TILE), 1)) < t_col
    diag_mask = _tile_rows(diag_mask, 2)
    qms = []
    for p in range(SB_PAIRS):
        q = q_ref[0, :, p * LANES:(p + 1) * LANES]
        qms.append(jnp.concatenate(
            [jnp.where((lane // HEAD_DIM) == hl, q, jnp.zeros((), BF16)) for hl in range(2)], axis=0))

    def tiles(k0, state, mask):
        pairs = range(SB_PAIRS)
        cols = [slice(p * LANES, (p + 1) * LANES) for p in pairs]
        z2 = [_dot_nt(qms[p], k_ref[0, pl.ds(k0, SB_TILE), cols[p]]) * LOG2E for p in pairs]
        split = []
        for p in pairs:
            neg_abs = pltpu.bitcast(pltpu.bitcast(z2[p], jnp.uint32) | jnp.uint32(0x80000000), F32)
            nfail = jnp.maximum(z2[p], 0.0) + jnp.log2(1.0 + jnp.exp2(neg_abs))
            if mask is not None:
                nfail = jnp.where(mask, nfail, 0.0)
            split.append(jnp.concatenate(_split_bf16(nfail), axis=1))
        tail = [_dot(split[p], u_ref[...]) for p in pairs]
        a = []
        for p in pairs:
            ap = jnp.exp2(z2[p] + tail[p] + state[p][0])
            if mask is not None:
                ap = jnp.where(mask, ap, 0.0)
            a.append(ap.astype(BF16))
        return tuple((state[p][0] + tail[p][:, 0:1],
                      state[p][1] + _dot(a[p], v_ref[0, pl.ds(k0, SB_TILE), cols[p]])) for p in pairs)

    zero = (jnp.zeros((2 * SB_TILE, 1), F32), jnp.zeros((2 * SB_TILE, LANES), F32))
    state = tiles(pl.multiple_of(t0, SB_TILE), (zero,) * SB_PAIRS, diag_mask)

    def cond(c):
        i, state = c
        top = functools.reduce(jnp.maximum, [run for run, _ in state])
        return jnp.logical_and(i <= qi, jnp.max(top) > -SB_SKIP)

    def body(c):
        i, state = c
        k0 = pl.multiple_of((qi - i) * SB_TILE, SB_TILE)
        return i + 1, tiles(k0, state, None)

    _, state = lax.while_loop(cond, body, (jnp.int32(1), state))
    for p, (_, acc) in enumerate(state):
        o_ref[0, :, p * LANES:(p + 1) * LANES] = jnp.where(
            lane < HEAD_DIM, acc[:SB_TILE], acc[SB_TILE:]).astype(o_ref.dtype)


def _sb_attention(qb, kb, vb, upper):
    b, s, w = qb.shape
    wb = SB_PAIRS * LANES
    return pl.pallas_call(
        _sb_kernel,
        grid=(b, w // wb, s // SB_TILE),
        in_specs=[pl.BlockSpec((1, SB_TILE, wb), lambda i, j, t: (i, t, j)),
                  pl.BlockSpec((1, s, wb), lambda i, j, t: (i, 0, j)),
                  pl.BlockSpec((1, s, wb), lambda i, j, t: (i, 0, j)),
                  _resident(upper)],
        out_specs=pl.BlockSpec((1, SB_TILE, wb), lambda i, j, t: (i, t, j)),
        out_shape=jax.ShapeDtypeStruct((b, s, w), BF16),
        compiler_params=pltpu.CompilerParams(
            dimension_semantics=("parallel", "parallel", "parallel"), vmem_limit_bytes=VMEM_LIMIT),
        name="sb_attention",
    )(qb, kb, vb, upper)


MERGE_TM = 512
MERGE_CHAINS = 2


def _merge_kernel(oa_ref, ob_ref, xb_ref, x_ref, wg_ref, wa_ref, wb_ref, wo_ref, g_ref, b_ref, y_ref, ybf_ref):
    half = xb_ref.shape[0] // MERGE_CHAINS
    rows = [slice(c * half, (c + 1) * half) for c in range(MERGE_CHAINS)]
    ya = [_dot(oa_ref[r, :], wa_ref[...]) for r in rows]
    ga = [_dot(xb_ref[r, :], wg_ref[:, :D_MODEL]) for r in rows]
    merged = [jax.nn.sigmoid(g_c) * y_c for g_c, y_c in zip(ga, ya)]
    yb = [_dot(ob_ref[r, :], wb_ref[...]) for r in rows]
    gb = [_dot(xb_ref[r, :], wg_ref[:, D_MODEL:]) for r in rows]
    merged = [m_c + jax.nn.sigmoid(g_c) * y_c for m_c, g_c, y_c in zip(merged, gb, yb)]
    y = [_dot(m_c.astype(BF16), wo_ref[...]) for m_c in merged]
    for r, y_c in zip(rows, y):
        out = _layer_norm(DEEPNORM_ALPHA * x_ref[r, :] + y_c, g_ref[...], b_ref[...])
        y_ref[r, :] = out
        ybf_ref[r, :] = out.astype(BF16)


def _merge(oa, ob, x_bf, x, layer, wg, wa, wb, wo, g, b):
    t = x.shape[0]
    tm = MERGE_TM
    row = lambda w: pl.BlockSpec((tm, w), lambda i: (i, 0))
    full = lambda a: _layer_block(a, layer)
    gate_cols = pl.BlockSpec((None, D_MODEL, _GATE_WIDTH), lambda i: (layer, 0, 0), pipeline_mode=pl.Buffered(1))
    return pl.pallas_call(
        _merge_kernel,
        grid=(t // tm,),
        in_specs=[row(NSA_Q), row(SB_W), row(D_MODEL), row(D_MODEL),
                  gate_cols, full(wa), full(wb), full(wo), full(g), full(b)],
        out_specs=[row(D_MODEL), row(D_MODEL)],
        out_shape=[jax.ShapeDtypeStruct((t, D_MODEL), F32), jax.ShapeDtypeStruct((t, D_MODEL), BF16)],
        compiler_params=pltpu.CompilerParams(
            dimension_semantics=("parallel",), vmem_limit_bytes=VMEM_LIMIT),
        name="merge_out_ln",
    )(oa, ob, x_bf, x, wg, wa, wb, wo, g, b)


FFN_TM = 512
FFN_CHUNK = 256
HALO = 16


def _ffn_kernel(h_ref, halo_ref, x_ref, wu_ref, cw_ref, cb_ref, wd_ref, g_ref, b_ref, y_ref, ybf_ref,
                hx_ref, buf_ref, act_ref, *, tiles_per_seq):
    i = pl.program_id(0)
    has_prev = (i % tiles_per_seq) != 0
    tm = h_ref.shape[0]
    n_chunks = D_FF // FFN_CHUNK
    hx_ref[0:HALO, :] = jnp.where(has_prev, halo_ref[...], jnp.zeros((), BF16))
    hx_ref[HALO:, :] = h_ref[...]

    def up(c):
        for half in range(2):
            cols = slice(half * D_FF + c * FFN_CHUNK, half * D_FF + (c + 1) * FFN_CHUNK)
            buf_ref[half, c % 2] = _dot(hx_ref[...], wu_ref[:, cols])

    def gate(c):
        def conv(half):
            cols = slice(half * D_FF + c * FFN_CHUNK, half * D_FF + (c + 1) * FFN_CHUNK)
            u = buf_ref.at[half, c % 2]
            return (cw_ref[0:1, cols] * u[HALO - 2:HALO - 2 + tm, :] + cw_ref[1:2, cols] * u[HALO - 1:HALO - 1 + tm, :]
                    + cw_ref[2:3, cols] * u[HALO:HALO + tm, :] + cb_ref[:, cols])
        a = conv(0)
        act_ref[:, c * FFN_CHUNK:(c + 1) * FFN_CHUNK] = (a * jax.nn.sigmoid(a) * conv(1)).astype(BF16)

    up(0)
    for c in range(1, n_chunks):
        up(c)
        gate(c - 1)
    gate(n_chunks - 1)
    y = _dot(act_ref[...], wd_ref[...])
    out = _layer_norm(DEEPNORM_ALPHA * x_ref[...] + y, g_ref[...], b_ref[...])
    y_ref[...] = out
    ybf_ref[...] = out.astype(BF16)


def _conv_ffn(h_bf, x, layer, w_up, conv_w, conv_b, w_down, g, b, seq):
    t = x.shape[0]
    tm = FFN_TM
    row = lambda w: pl.BlockSpec((tm, w), lambda i: (i, 0))
    return pl.pallas_call(
        functools.partial(_ffn_kernel, tiles_per_seq=seq // tm),
        grid=(t // tm,),
        in_specs=[row(D_MODEL),
                  pl.BlockSpec((HALO, D_MODEL), lambda i: (jnp.maximum(i * (tm // HALO) - 1, 0), 0)),
                  row(D_MODEL),
                  *[_layer_block(a, layer) for a in (w_up, conv_w, conv_b, w_down, g, b)]],
        out_specs=[row(D_MODEL), row(D_MODEL)],
        out_shape=[jax.ShapeDtypeStruct((t, D_MODEL), F32), jax.ShapeDtypeStruct((t, D_MODEL), BF16)],
        scratch_shapes=[pltpu.VMEM((tm + HALO, D_MODEL), BF16),
                        pltpu.VMEM((2, 2, tm + HALO, FFN_CHUNK), F32), pltpu.VMEM((tm, D_FF), BF16)],
        compiler_params=pltpu.CompilerParams(
            dimension_semantics=("parallel",), vmem_limit_bytes=VMEM_LIMIT),
        name="conv_ffn_ln",
    )(h_bf, h_bf, x, w_up, conv_w, conv_b, w_down, g, b)


def _rope_tables(seq):
    inv_freq = ROPE_THETA ** (-np.arange(0, ROT_DIM, 2, dtype=np.float32) / ROT_DIM)
    ang = jnp.arange(seq, dtype=F32)[:, None] * jnp.asarray(inv_freq, F32)[None, :]
    cos, sin = jnp.cos(ang), jnp.sin(ang)
    half = ROT_DIM // 2
    ones = jnp.ones((seq, HEAD_DIM - ROT_DIM), F32)
    zeros = jnp.zeros((seq, HEAD_DIM - ROT_DIM), F32)
    zh = jnp.zeros((seq, half), F32)
    cos_t = jnp.concatenate([cos, cos, ones], axis=1)
    sin_prev = jnp.concatenate([zh, sin, zeros], axis=1)
    sin_next = jnp.concatenate([-sin, zh, zeros], axis=1)
    rep = LANES // HEAD_DIM
    return tuple(jnp.tile(a, (1, rep)) for a in (cos_t, sin_prev, sin_next))


_QA_PERM = np.concatenate([np.concatenate([np.arange(h * HEAD_DIM, (h + 1) * HEAD_DIM),
                                           np.arange((NSA_GROUP + h) * HEAD_DIM, (NSA_GROUP + h + 1) * HEAD_DIM)])
                           for h in range(NSA_GROUP)])


def _selection_overlap_t(seq):
    n_cmp = (seq - CMP_BLOCK) // CMP_STRIDE + 1
    n_sel = seq // SEL_BLOCK
    cs = np.arange(n_cmp) * CMP_STRIDE
    ce = cs + CMP_BLOCK
    ss = np.arange(n_sel) * SEL_BLOCK
    se = ss + SEL_BLOCK
    ov = np.clip(np.minimum(ce[:, None], se[None, :]) - np.maximum(cs[:, None], ss[None, :]), 0, None) / CMP_BLOCK
    out = np.zeros((n_sel, seq // CMP_STRIDE), np.float32)
    out[:, :n_cmp] = ov.T
    return jnp.asarray(out, BF16)


def _block_indicator(seq):
    e = np.zeros((seq, LANES), np.float32)
    e[np.arange(seq), np.arange(seq) // SEL_BLOCK] = 1.0
    return jnp.asarray(e, BF16)


def _gate_expand():
    x = np.zeros((LANES, 3 * NSA_Q), np.float32)
    for col in range(NSA_Q):
        hh = _QA_PERM[col] // HEAD_DIM
        for r in range(3):
            x[hh * 3 + r, r * NSA_Q + col] = 1.0
    return jnp.asarray(x, BF16)


def _neg_later_keys(n):
    u = -(np.arange(n)[:, None] >= np.arange(n)[None, :]).astype(np.float32)
    return jnp.asarray(np.concatenate([u, u], axis=0), BF16)


def _heads_to_qa_order(a, axis):
    shape = a.shape
    a = a.reshape(shape[:axis] + (NSA_KV_HEADS, NSA_GROUP, HEAD_DIM) + shape[axis + 1:])
    return jnp.swapaxes(a, axis, axis + 1).reshape(shape)


def _prep_in_proj(w):
    q_a, kc, vc, ks, vs, kw, vw, g_a, q_b, k_b, v_b, g_m = jnp.split(w.astype(BF16), SPLIT_POINTS, axis=-1)
    g_a = jnp.pad(g_a, ((0, 0), (0, 0), (0, LANES - g_a.shape[-1])))
    cols = [g_m, _heads_to_qa_order(q_a, 2), kc, ks, kw, vc, vs, vw, q_b, k_b, v_b, g_a]
    return jnp.concatenate(cols, axis=-1)


def _prep_compress(pos, w1, b1, w2):
    half = CMP_BLOCK // 2
    pos2 = pos.reshape(2, half, 1, HEAD_DIM)
    pos_t = jnp.broadcast_to(pos2, (2, half, NSA_KV_HEADS, HEAD_DIM)).reshape(2, half * NSA_KV)
    w1r = w1.reshape(2, half, HEAD_DIM, CMP_HIDDEN)
    eye = jnp.eye(NSA_KV_HEADS, dtype=w1.dtype)
    w1x = (w1r[:, :, None, :, None, :] * eye[None, None, :, None, :, None]).reshape(
        2, half * NSA_KV, NSA_KV_HEADS * CMP_HIDDEN)
    b1t = jnp.tile(b1.reshape(1, CMP_HIDDEN), (1, NSA_KV_HEADS))
    w2x = (w2[None, :, None, :] * eye[:, None, :, None]).reshape(NSA_KV_HEADS * CMP_HIDDEN, NSA_KV)
    return pos_t, w1x[0].astype(BF16), w1x[1].astype(BF16), b1t, w2x.astype(BF16)


def kernel(x, w_in, cmp_pos_k, cmp_w1_k, cmp_b1_k, cmp_w2_k, cmp_pos_v, cmp_w1_v, cmp_b1_v, cmp_w2_v,
           w_branch_a, w_branch_b, w_out, ln_mix_g, ln_mix_b, w_up, conv_w, conv_b, w_down,
           ln_ffn_g, ln_ffn_b):
    b, s, d = x.shape
    t = b * s
    assert d == D_MODEL and s // SEL_BLOCK == N_SEL and s // CMP_STRIDE == LANES
    cos_t, sin_prev, sin_next = _rope_tables(s)
    ovt = _selection_overlap_t(s)
    expand = _block_indicator(s)
    gate_x = _gate_expand()
    upper = _neg_later_keys(SB_TILE)

    w_proj = _prep_in_proj(w_in)
    wa = _heads_to_qa_order(w_branch_a.astype(BF16), 1)
    wb = w_branch_b.astype(BF16)
    wo = w_out.astype(BF16)
    wu = w_up.astype(BF16)
    wd = w_down.astype(BF16)
    stack_row = lambda a: a.reshape(a.shape[0], 1, a.shape[1])
    cb, g_mix, b_mix, g_ffn, b_ffn = map(stack_row, (conv_b, ln_mix_g, ln_mix_b, ln_ffn_g, ln_ffn_b))

    xf = x.reshape(t, d)
    xb = xf.astype(BF16)
    for l in range(DEPTH):
        outs = _in_proj(xb, w_proj, l, cos_t, sin_prev, sin_next, s)
        qa, kcmp, ksel, kwin, vcmp, vsel, vwin, qb, kb, vb, ga = outs
        rows = s // CMP_STRIDE
        kc = _compress(kcmp.reshape(b, rows, CMP_STRIDE * LANES),
                       *_prep_compress(cmp_pos_k[l], cmp_w1_k[l], cmp_b1_k[l], cmp_w2_k[l]))
        vc = _compress(vcmp.reshape(b, rows, CMP_STRIDE * LANES),
                       *_prep_compress(cmp_pos_v[l], cmp_w1_v[l], cmp_b1_v[l], cmp_w2_v[l]))
        r3 = lambda a: a.reshape(b, s, a.shape[-1])
        oa = _nsa_attention(r3(qa), kc, vc, r3(ksel), r3(vsel), r3(kwin), r3(vwin), r3(ga),
                            ovt, expand, gate_x)
        ob = _sb_attention(r3(qb), r3(kb), r3(vb), upper)
        xf, xb = _merge(oa.reshape(t, NSA_Q), ob.reshape(t, SB_W), xb, xf, l,
                        w_proj, wa, wb, wo, g_mix, b_mix)
        xf, xb = _conv_ffn(xb, xf, l, wu, conv_w, cb, wd, g_ffn, b_ffn, s)
    return xf.reshape(b, s, d)
```

```python
import functools

import numpy as np
import jax
import jax.numpy as jnp
from jax import lax
from jax.experimental import pallas as pl
from jax.experimental.pallas import tpu as pltpu

D_MODEL = 1024
DEPTH = 4
HEAD_DIM = 64
NSA_HEADS = 8
NSA_KV_HEADS = 2
NSA_GROUP = NSA_HEADS // NSA_KV_HEADS
SB_HEADS = 8
CMP_BLOCK = 32
CMP_STRIDE = 16
CMP_HIDDEN = 128
SEL_BLOCK = 64
SEL_TOPK = 16
WINDOW = 512
ROPE_THETA = 500000.0
ROT_DIM = HEAD_DIM // 4
D_FF = 2816
CONV_W = 3
LN_EPS = 1e-5
NEG = -1e30
FORCE = 1e4
DEEPNORM_ALPHA = (2.0 * DEPTH) ** 0.25

NSA_Q = NSA_HEADS * HEAD_DIM
NSA_KV = NSA_KV_HEADS * HEAD_DIM
SB_W = SB_HEADS * HEAD_DIM
SPLIT_SIZES = (NSA_Q, NSA_KV, NSA_KV, NSA_KV, NSA_KV, NSA_KV, NSA_KV, 3 * NSA_HEADS, SB_W, SB_W, SB_W, 2 * D_MODEL)
SPLIT_POINTS = tuple(int(v) for v in np.cumsum(SPLIT_SIZES)[:-1])

LANES = 128
VMEM_LIMIT = 56 * 1024 * 1024

BF16 = jnp.bfloat16
F32 = jnp.float32

Q_TILE = 128
SEL_KEY_TILE = 512
SB_TILE = 256
N_SEL = 32
SEL_CHAINS = 2


def _dot(a, b):
    return jnp.dot(a, b, preferred_element_type=F32)


def _dot_nt(a, b):
    return lax.dot_general(a, b, (((1,), (1,)), ((), ())), preferred_element_type=F32)


def _split_bf16(x):
    hi = x.astype(BF16)
    lo = (x - hi.astype(F32)).astype(BF16)
    return hi, lo


def _layer_norm(z, g, b):
    mu = jnp.mean(z, axis=-1, keepdims=True)
    zc = z - mu
    var = jnp.mean(zc * zc, axis=-1, keepdims=True)
    return zc * lax.rsqrt(var + LN_EPS) * g + b


def _resident(a):
    return pl.BlockSpec(a.shape, lambda *_: (0,) * a.ndim, pipeline_mode=pl.Buffered(1))


def _layer_block(a, l):
    return pl.BlockSpec((None,) + a.shape[1:], lambda *_: (l,) + (0,) * (a.ndim - 1), pipeline_mode=pl.Buffered(1))


_PROJ_OUT = (
    ("qa", NSA_Q, BF16, True, False, HEAD_DIM ** -0.5),
    ("kcmp", LANES, F32, True, False, 1.0),
    ("ksel", LANES, BF16, True, False, 1.0),
    ("kwin", LANES, BF16, True, False, 1.0),
    ("vcmp", LANES, F32, False, False, 1.0),
    ("vsel", LANES, BF16, False, False, 1.0),
    ("vwin", LANES, BF16, False, False, 1.0),
    ("qb", SB_W, BF16, False, False, HEAD_DIM ** -0.5),
    ("kb", SB_W, BF16, False, False, 1.0),
    ("vb", SB_W, BF16, False, False, 1.0),
    ("ga", LANES, F32, False, True, 1.0),
)
_PROJ_WIDTH = sum(o[1] for o in _PROJ_OUT)
_ROWS16 = ("kcmp", "vcmp")
_GATE_WIDTH = 2 * D_MODEL
PROJ_TM = 512


def _rope(acc, cos_t, sin_prev, sin_next):
    chunks = []
    for c in range(acc.shape[1] // LANES):
        xc = acc[:, c * LANES:(c + 1) * LANES]
        chunks.append(xc * cos_t + pltpu.roll(xc, ROT_DIM // 2, 1) * sin_prev
                      + pltpu.roll(xc, LANES - ROT_DIM // 2, 1) * sin_next)
    return chunks[0] if len(chunks) == 1 else jnp.concatenate(chunks, axis=1)


def _proj_kernel(x_ref, w_ref, cos_ref, sp_ref, sn_ref, *refs):
    out_refs, stage_ref = refs[:-1], refs[-1]
    x = x_ref[...]
    tm = x.shape[0]

    def finish(acc, spec, o_ref):
        name, _, dtype, rotary, sigmoid, scale = spec
        if rotary:
            acc = _rope(acc, cos_ref[...], sp_ref[...], sn_ref[...])
        if sigmoid:
            acc = jax.nn.sigmoid(acc)
        if scale != 1.0:
            acc = acc * scale
        if name in _ROWS16:
            stage_ref[...] = acc
            for l in range(CMP_STRIDE):
                o_ref[:, l * LANES:(l + 1) * LANES] = stage_ref[pl.ds(l, tm // CMP_STRIDE, stride=CMP_STRIDE), :]
        else:
            o_ref[...] = acc.astype(dtype)

    col = _GATE_WIDTH
    pending = None
    for spec, o_ref in zip(_PROJ_OUT, out_refs):
        acc = _dot(x, w_ref[:, col:col + spec[1]])
        if pending is not None:
            finish(*pending)
        pending = (acc, spec, o_ref)
        col += spec[1]
    finish(*pending)


def _in_proj(x_bf, w, layer, cos_t, sin_prev, sin_next, seq):
    t = x_bf.shape[0]
    tm = PROJ_TM
    tiles_per_seq = seq // tm
    tab_spec = pl.BlockSpec((tm, LANES), lambda i: (i % tiles_per_seq, 0))
    fold = lambda o: CMP_STRIDE if o[0] in _ROWS16 else 1
    return pl.pallas_call(
        _proj_kernel,
        grid=(t // tm,),
        in_specs=[
            pl.BlockSpec((tm, D_MODEL), lambda i: (i, 0)),
            _layer_block(w, layer),
            tab_spec, tab_spec, tab_spec,
        ],
        out_specs=[pl.BlockSpec((tm // fold(o), o[1] * fold(o)), lambda i: (i, 0)) for o in _PROJ_OUT],
        out_shape=[jax.ShapeDtypeStruct((t // fold(o), o[1] * fold(o)), o[2]) for o in _PROJ_OUT],
        scratch_shapes=[pltpu.VMEM((tm, LANES), F32)],
        compiler_params=pltpu.CompilerParams(
            dimension_semantics=("parallel",), vmem_limit_bytes=VMEM_LIMIT),
        name="in_proj",
    )(x_bf, w, cos_t, sin_prev, sin_next)


def _gelu_tanh(x):
    return 0.5 * x * (1.0 + jnp.tanh(np.sqrt(2.0 / np.pi).astype(np.float32) * (x + 0.044715 * (x * x * x))))


def _compress_kernel(r_ref, pos_ref, w1a_ref, w1b_ref, b1_ref, w2_ref, o_ref):
    r = r_ref[0]
    ra = (r + pos_ref[0:1, :]).astype(BF16)
    rb = (r + pos_ref[1:2, :]).astype(BF16)
    p1 = _dot(ra, w1a_ref[...])
    p2 = _dot(rb, w1b_ref[...])
    nrow = p2.shape[0]
    hid = _gelu_tanh(p1 + pltpu.roll(p2, nrow - 1, 0) + b1_ref[...])
    o_ref[0] = _dot(hid.astype(BF16), w2_ref[...]).astype(o_ref.dtype)


def _compress(r, pos, w1a, w1b, b1, w2):
    b, nrow, width = r.shape
    full = _resident
    return pl.pallas_call(
        _compress_kernel,
        grid=(b,),
        in_specs=[pl.BlockSpec((1, nrow, width), lambda i: (i, 0, 0)),
                  full(pos), full(w1a), full(w1b), full(b1), full(w2)],
        out_specs=pl.BlockSpec((1, nrow, LANES), lambda i: (i, 0, 0)),
        out_shape=jax.ShapeDtypeStruct((b, nrow, LANES), BF16),
        compiler_params=pltpu.CompilerParams(
            dimension_semantics=("parallel",), vmem_limit_bytes=VMEM_LIMIT),
        name="compress",
    )(r, pos, w1a, w1b, b1, w2)


def _masked_softmax(s, mask):
    s = jnp.where(mask, s, NEG)
    m = jnp.max(s, axis=-1, keepdims=True)
    e = jnp.where(mask, jnp.exp(s - m), 0.0)
    return e / jnp.maximum(jnp.sum(e, axis=-1, keepdims=True), 1e-30)


def _tile_rows(a, n):
    return jnp.concatenate([a] * n, axis=0)


def _nsa_kernel(q_ref, kc_ref, vc_ref, ks_ref, vs_ref, kw_ref, vw_ref, ga_ref,
                ovt_ref, et_ref, x_ref, o_ref, osel_ref):
    qi = pl.program_id(1)
    t0 = qi * Q_TILE
    hq = NSA_GROUP * Q_TILE
    nrows = NSA_HEADS * Q_TILE
    lane = lax.broadcasted_iota(jnp.int32, (1, LANES), 1)
    t_col = t0 + lax.broadcasted_iota(jnp.int32, (Q_TILE, 1), 0)
    t_rows = _tile_rows(t_col, NSA_HEADS)
    q_all = q_ref[0]

    ga_hi, ga_lo = _split_bf16(ga_ref[0])
    gates = _dot(ga_hi, x_ref[...]) + _dot(ga_lo, x_ref[...])

    j_col = lax.broadcasted_iota(jnp.int32, (N_SEL, 1), 0)
    t_lane = t0 + lax.broadcasted_iota(jnp.int32, (1, Q_TILE), 1)
    cur = t_lane // SEL_BLOCK
    forced = (j_col == 0) | (j_col == cur) | (j_col == cur - 1)
    valid = j_col * SEL_BLOCK <= t_lane

    w0 = pl.multiple_of(jnp.maximum(t0 - WINDOW, 0), Q_TILE)
    wlen = WINDOW + Q_TILE
    diff = t_col - (w0 + lax.broadcasted_iota(jnp.int32, (1, wlen), 1))
    win_bias = _tile_rows(jnp.where((diff >= 0) & (diff < WINDOW), 0.0, NEG), NSA_HEADS)
    kw = kw_ref[0, pl.ds(w0, wlen), :]
    vw = vw_ref[0, pl.ds(w0, wlen), :]

    last = qi // (SEL_KEY_TILE // Q_TILE)

    qs = jnp.concatenate(
        [jnp.where((lane // HEAD_DIM) == g, q_all[:, h * LANES:(h + 1) * LANES], jnp.zeros((), BF16))
         for g in range(NSA_KV_HEADS) for h in range(NSA_GROUP)], axis=0)

    s = _dot_nt(qs, kc_ref[0])
    m_cmp = (CMP_STRIDE * lane + CMP_BLOCK - 1) <= t_rows
    p = _masked_softmax(s, m_cmp)
    o_cmp = _dot(p.astype(BF16), vc_ref[0])

    sw = _dot_nt(qs, kw) + win_bias
    ew = jnp.exp(sw - jnp.max(sw, axis=-1, keepdims=True)).astype(BF16)
    ow = _dot(ew, jnp.concatenate([vw, jnp.ones((wlen, LANES), BF16)], axis=1))
    o_win = ow[:, :LANES] / jnp.maximum(ow[:, LANES:], 1e-30)

    drops = []
    for g in range(NSA_KV_HEADS):
        pg = p[g * hq:(g + 1) * hq]
        psum = pg[0:Q_TILE] + pg[Q_TILE:2 * Q_TILE] + pg[2 * Q_TILE:3 * Q_TILE] + pg[3 * Q_TILE:4 * Q_TILE]
        ps_hi, ps_lo = _split_bf16(psum)
        score = _dot_nt(ovt_ref[...], ps_hi) + _dot_nt(ovt_ref[...], ps_lo)
        score = jnp.where(forced, FORCE, jnp.where(valid, score, -FORCE))
        rank = jnp.zeros((N_SEL, Q_TILE), F32)
        for jp in range(N_SEL):
            row = score[jp:jp + 1, :]
            ge = jnp.where(row >= score, 1.0, 0.0)
            gt = jnp.where(row > score, 1.0, 0.0)
            rank = rank + jnp.where(j_col > jp, ge, gt)
        drop_t = jnp.where(rank < SEL_TOPK, 0.0, NEG)
        drop_t = jnp.concatenate([drop_t, jnp.zeros((LANES - N_SEL, Q_TILE), F32)], axis=0)
        drop = drop_t.T.astype(BF16)
        drops.append(_tile_rows(drop, NSA_GROUP))
    q_aug = jnp.concatenate([qs, jnp.concatenate(drops, axis=0)], axis=1)

    crow = nrows // SEL_CHAINS
    q_parts = [q_aug[c * crow:(c + 1) * crow] for c in range(SEL_CHAINS)]

    def sel_tile(k0, carry, mask):
        k_aug = jnp.concatenate([ks_ref[0, pl.ds(k0, SEL_KEY_TILE), :],
                                 et_ref[pl.ds(k0, SEL_KEY_TILE), :]], axis=1)
        v_aug = jnp.concatenate([vs_ref[0, pl.ds(k0, SEL_KEY_TILE), :],
                                 jnp.ones((SEL_KEY_TILE, LANES), BF16)], axis=1)
        sc = [_dot_nt(q, k_aug) for q in q_parts]
        if mask is not None:
            sc = [jnp.where(mask, s_c, NEG) for s_c in sc]
        m_new = [jnp.maximum(m_i, jnp.max(s_c, axis=-1, keepdims=True)) for s_c, (m_i, _) in zip(sc, carry)]
        e = [jnp.exp(s_c - m_c).astype(BF16) for s_c, m_c in zip(sc, m_new)]
        return tuple((m_c, jnp.exp(m_i - m_c) * acc + _dot(e_c, v_aug))
                     for m_c, e_c, (m_i, acc) in zip(m_new, e, carry))

    for n_tiles in range(1, et_ref.shape[0] // SEL_KEY_TILE + 1):
        @pl.when(last == n_tiles - 1)
        def _():
            carry = ((jnp.full((crow, 1), NEG, F32), jnp.zeros((crow, 2 * LANES), F32)),) * SEL_CHAINS
            for kt in range(n_tiles - 1):
                carry = sel_tile(kt * SEL_KEY_TILE, carry, None)
            k_diag = (n_tiles - 1) * SEL_KEY_TILE
            causal = _tile_rows((k_diag + lax.broadcasted_iota(jnp.int32, (1, SEL_KEY_TILE), 1)) <= t_col,
                                crow // Q_TILE)
            acc = jnp.concatenate([a for _, a in sel_tile(k_diag, carry, causal)], axis=0)
            osel_ref[...] = acc[:, :LANES] / jnp.maximum(acc[:, LANES:], 1e-30)
    o_sel = osel_ref[...]

    branches = (o_cmp, o_sel, o_win)
    for h in range(NSA_GROUP):
        mixed = []
        for g in range(NSA_KV_HEADS):
            r0 = (g * NSA_GROUP + h) * Q_TILE
            acc = None
            for r in range(3):
                term = gates[:, r * NSA_Q + h * LANES: r * NSA_Q + (h + 1) * LANES] * branches[r][r0:r0 + Q_TILE]
                acc = term if acc is None else acc + term
            mixed.append(acc)
        o_ref[0, :, h * LANES:(h + 1) * LANES] = jnp.where(lane < HEAD_DIM, mixed[0], mixed[1]).astype(o_ref.dtype)


def _nsa_attention(qa, kc, vc, ksel, vsel, kwin, vwin, ga, ovt, expand, gate_x):
    b, s, _ = qa.shape
    per_q = lambda w: pl.BlockSpec((1, Q_TILE, w), lambda i, j: (i, j, 0))
    per_b = lambda a: pl.BlockSpec((1,) + a.shape[1:], lambda i, j: (i, 0, 0))
    full = _resident
    return pl.pallas_call(
        _nsa_kernel,
        grid=(b, s // Q_TILE),
        in_specs=[per_q(NSA_Q), per_b(kc), per_b(vc), per_b(ksel), per_b(vsel), per_b(kwin), per_b(vwin),
                  per_q(LANES), full(ovt), full(expand), full(gate_x)],
        out_specs=per_q(NSA_Q),
        out_shape=jax.ShapeDtypeStruct((b, s, NSA_Q), BF16),
        scratch_shapes=[pltpu.VMEM((NSA_HEADS * Q_TILE, LANES), F32)],
        compiler_params=pltpu.CompilerParams(
            dimension_semantics=("parallel", "parallel"), vmem_limit_bytes=VMEM_LIMIT),
        name="nsa_attention",
    )(qa, kc, vc, ksel, vsel, kwin, vwin, ga, ovt, expand, gate_x)


SB_SKIP = 150.0
LOG2E = 1.4426950408889634
SB_PAIRS = 4


def _sb_kernel(q_ref, k_ref, v_ref, u_ref, o_ref, run_ref, acc_ref):
    qi = pl.program_id(2)
    t0 = qi * SB_TILE
    lane = lax.broadcasted_iota(jnp.int32, (1, LANES), 1)
    t_col = t0 + lax.broadcasted_iota(jnp.int32, (SB_TILE, 1), 0)
    diag_mask = (t0 + lax.broadcasted_iota(jnp.int32, (1, SB_TILE), 1)) < t_col
    diag_mask = _tile_rows(diag_mask, 2)
    qms = []
    for p in range(SB_PAIRS):
        q = q_ref[0, :, p * LANES:(p + 1) * LANES]
        qms.append(jnp.concatenate(
            [jnp.where((lane // HEAD_DIM) == hl, q, jnp.zeros((), BF16)) for hl in range(2)], axis=0))

    def tiles(k0, state, mask):
        pairs = range(SB_PAIRS)
        cols = [slice(p * LANES, (p + 1) * LANES) for p in pairs]
        z2 = [_dot_nt(qms[p], k_ref[0, pl.ds(k0, SB_TILE), cols[p]]) * LOG2E for p in pairs]
        split = []
        for p in pairs:
            neg_abs = pltpu.bitcast(pltpu.bitcast(z2[p], jnp.uint32) | jnp.uint32(0x80000000), F32)
            nfail = jnp.maximum(z2[p], 0.0) + jnp.log2(1.0 + jnp.exp2(neg_abs))
            if mask is not None:
                nfail = jnp.where(mask, nfail, 0.0)
            split.append(jnp.concatenate(_split_bf16(nfail), axis=1))
        tail = [_dot(split[p], u_ref[...]) for p in pairs]
        a = []
        for p in pairs:
            ap = jnp.exp2(z2[p] + tail[p] + state[p][0])
            if mask is not None:
                ap = jnp.where(mask, ap, 0.0)
            a.append(ap.astype(BF16))
        return tuple((state[p][0] + tail[p][:, 0:1],
                      state[p][1] + _dot(a[p], v_ref[0, pl.ds(k0, SB_TILE), cols[p]])) for p in pairs)

    zero = (jnp.zeros((2 * SB_TILE, 1), F32), jnp.zeros((2 * SB_TILE, LANES), F32))

    def save(state):
        for p, (run, acc) in enumerate(state):
            run_ref[p] = run
            acc_ref[p] = acc

    @pl.when(qi == 0)
    def _():
        save(tiles(0, (zero,) * SB_PAIRS, diag_mask))

    @pl.when(qi > 0)
    def _():
        state = tiles(pl.multiple_of(t0, SB_TILE), (zero,) * SB_PAIRS, diag_mask)
        save(tiles(pl.multiple_of(t0 - SB_TILE, SB_TILE), state, None))

    state = tuple((run_ref[p], acc_ref[p]) for p in range(SB_PAIRS))

    def cond(c):
        i, state = c
        top = functools.reduce(jnp.maximum, [run for run, _ in state])
        return jnp.logical_and(i <= qi, jnp.max(top) > -SB_SKIP)

    def body(c):
        i, state = c
        k0 = pl.multiple_of((qi - i) * SB_TILE, SB_TILE)
        return i + 1, tiles(k0, state, None)

    _, state = lax.while_loop(cond, body, (jnp.int32(2), state))
    for p, (_, acc) in enumerate(state):
        o_ref[0, :, p * LANES:(p + 1) * LANES] = jnp.where(
            lane < HEAD_DIM, acc[:SB_TILE], acc[SB_TILE:]).astype(o_ref.dtype)


def _sb_attention(qb, kb, vb, upper):
    b, s, w = qb.shape
    wb = SB_PAIRS * LANES
    return pl.pallas_call(
        _sb_kernel,
        grid=(b, w // wb, s // SB_TILE),
        in_specs=[pl.BlockSpec((1, SB_TILE, wb), lambda i, j, t: (i, t, j)),
                  pl.BlockSpec((1, s, wb), lambda i, j, t: (i, 0, j)),
                  pl.BlockSpec((1, s, wb), lambda i, j, t: (i, 0, j)),
                  _resident(upper)],
        out_specs=pl.BlockSpec((1, SB_TILE, wb), lambda i, j, t: (i, t, j)),
        out_shape=jax.ShapeDtypeStruct((b, s, w), BF16),
        scratch_shapes=[pltpu.VMEM((SB_PAIRS, 2 * SB_TILE, 1), F32), pltpu.VMEM((SB_PAIRS, 2 * SB_TILE, LANES), F32)],
        compiler_params=pltpu.CompilerParams(
            dimension_semantics=("parallel", "parallel", "parallel"), vmem_limit_bytes=VMEM_LIMIT),
        name="sb_attention",
    )(qb, kb, vb, upper)


MERGE_TM = 512
MERGE_CHAINS = 2


def _merge_kernel(oa_ref, ob_ref, xb_ref, x_ref, wg_ref, wa_ref, wb_ref, wo_ref, g_ref, b_ref, y_ref, ybf_ref):
    half = xb_ref.shape[0] // MERGE_CHAINS
    rows = [slice(c * half, (c + 1) * half) for c in range(MERGE_CHAINS)]
    ya = [_dot(oa_ref[r, :], wa_ref[...]) for r in rows]
    ga = [_dot(xb_ref[r, :], wg_ref[:, :D_MODEL]) for r in rows]
    merged = [jax.nn.sigmoid(g_c) * y_c for g_c, y_c in zip(ga, ya)]
    yb = [_dot(ob_ref[r, :], wb_ref[...]) for r in rows]
    gb = [_dot(xb_ref[r, :], wg_ref[:, D_MODEL:]) for r in rows]
    merged = [m_c + jax.nn.sigmoid(g_c) * y_c for m_c, g_c, y_c in zip(merged, gb, yb)]
    y = [_dot(m_c.astype(BF16), wo_ref[...]) for m_c in merged]
    for r, y_c in zip(rows, y):
        out = _layer_norm(DEEPNORM_ALPHA * x_ref[r, :] + y_c, g_ref[...], b_ref[...])
        y_ref[r, :] = out
        ybf_ref[r, :] = out.astype(BF16)


def _merge(oa, ob, x_bf, x, layer, wg, wa, wb, wo, g, b):
    t = x.shape[0]
    tm = MERGE_TM
    row = lambda w: pl.BlockSpec((tm, w), lambda i: (i, 0))
    full = lambda a: _layer_block(a, layer)
    gate_cols = pl.BlockSpec((None, D_MODEL, _GATE_WIDTH), lambda i: (layer, 0, 0), pipeline_mode=pl.Buffered(1))
    return pl.pallas_call(
        _merge_kernel,
        grid=(t // tm,),
        in_specs=[row(NSA_Q), row(SB_W), row(D_MODEL), row(D_MODEL),
                  gate_cols, full(wa), full(wb), full(wo), full(g), full(b)],
        out_specs=[row(D_MODEL), row(D_MODEL)],
        out_shape=[jax.ShapeDtypeStruct((t, D_MODEL), F32), jax.ShapeDtypeStruct((t, D_MODEL), BF16)],
        compiler_params=pltpu.CompilerParams(
            dimension_semantics=("parallel",), vmem_limit_bytes=VMEM_LIMIT),
        name="merge_out_ln",
    )(oa, ob, x_bf, x, wg, wa, wb, wo, g, b)


FFN_TM = 512
FFN_CHUNK = 256
HALO = 16


def _ffn_kernel(h_ref, halo_ref, x_ref, wu_ref, cw_ref, cb_ref, wd_ref, g_ref, b_ref, y_ref, ybf_ref,
                hx_ref, buf_ref, act_ref, *, tiles_per_seq):
    i = pl.program_id(0)
    has_prev = (i % tiles_per_seq) != 0
    tm = h_ref.shape[0]
    n_chunks = D_FF // FFN_CHUNK
    hx_ref[0:HALO, :] = jnp.where(has_prev, halo_ref[...], jnp.zeros((), BF16))
    hx_ref[HALO:, :] = h_ref[...]

    def up(c):
        for half in range(2):
            cols = slice(half * D_FF + c * FFN_CHUNK, half * D_FF + (c + 1) * FFN_CHUNK)
            buf_ref[half, c % 2] = _dot(hx_ref[...], wu_ref[:, cols])

    def gate(c):
        def conv(half):
            cols = slice(half * D_FF + c * FFN_CHUNK, half * D_FF + (c + 1) * FFN_CHUNK)
            u = buf_ref.at[half, c % 2]
            return (cw_ref[0:1, cols] * u[HALO - 2:HALO - 2 + tm, :] + cw_ref[1:2, cols] * u[HALO - 1:HALO - 1 + tm, :]
                    + cw_ref[2:3, cols] * u[HALO:HALO + tm, :] + cb_ref[:, cols])
        a = conv(0)
        act_ref[:, c * FFN_CHUNK:(c + 1) * FFN_CHUNK] = (a * jax.nn.sigmoid(a) * conv(1)).astype(BF16)

    up(0)
    for c in range(1, n_chunks):
        up(c)
        gate(c - 1)
    gate(n_chunks - 1)
    y = _dot(act_ref[...], wd_ref[...])
    out = _layer_norm(DEEPNORM_ALPHA * x_ref[...] + y, g_ref[...], b_ref[...])
    y_ref[...] = out
    ybf_ref[...] = out.astype(BF16)


def _conv_ffn(h_bf, x, layer, w_up, conv_w, conv_b, w_down, g, b, seq):
    t = x.shape[0]
    tm = FFN_TM
    row = lambda w: pl.BlockSpec((tm, w), lambda i: (i, 0))
    return pl.pallas_call(
        functools.partial(_ffn_kernel, tiles_per_seq=seq // tm),
        grid=(t // tm,),
        in_specs=[row(D_MODEL),
                  pl.BlockSpec((HALO, D_MODEL), lambda i: (jnp.maximum(i * (tm // HALO) - 1, 0), 0)),
                  row(D_MODEL),
                  *[_layer_block(a, layer) for a in (w_up, conv_w, conv_b, w_down, g, b)]],
        out_specs=[row(D_MODEL), row(D_MODEL)],
        out_shape=[jax.ShapeDtypeStruct((t, D_MODEL), F32), jax.ShapeDtypeStruct((t, D_MODEL), BF16)],
        scratch_shapes=[pltpu.VMEM((tm + HALO, D_MODEL), BF16),
                        pltpu.VMEM((2, 2, tm + HALO, FFN_CHUNK), F32), pltpu.VMEM((tm, D_FF), BF16)],
        compiler_params=pltpu.CompilerParams(
            dimension_semantics=("parallel",), vmem_limit_bytes=VMEM_LIMIT),
        name="conv_ffn_ln",
    )(h_bf, h_bf, x, w_up, conv_w, conv_b, w_down, g, b)


def _rope_tables(seq):
    inv_freq = ROPE_THETA ** (-np.arange(0, ROT_DIM, 2, dtype=np.float32) / ROT_DIM)
    ang = jnp.arange(seq, dtype=F32)[:, None] * jnp.asarray(inv_freq, F32)[None, :]
    cos, sin = jnp.cos(ang), jnp.sin(ang)
    half = ROT_DIM // 2
    ones = jnp.ones((seq, HEAD_DIM - ROT_DIM), F32)
    zeros = jnp.zeros((seq, HEAD_DIM - ROT_DIM), F32)
    zh = jnp.zeros((seq, half), F32)
    cos_t = jnp.concatenate([cos, cos, ones], axis=1)
    sin_prev = jnp.concatenate([zh, sin, zeros], axis=1)
    sin_next = jnp.concatenate([-sin, zh, zeros], axis=1)
    rep = LANES // HEAD_DIM
    return tuple(jnp.tile(a, (1, rep)) for a in (cos_t, sin_prev, sin_next))


_QA_PERM = np.concatenate([np.concatenate([np.arange(h * HEAD_DIM, (h + 1) * HEAD_DIM),
                                           np.arange((NSA_GROUP + h) * HEAD_DIM, (NSA_GROUP + h + 1) * HEAD_DIM)])
                           for h in range(NSA_GROUP)])


def _selection_overlap_t(seq):
    n_cmp = (seq - CMP_BLOCK) // CMP_STRIDE + 1
    n_sel = seq // SEL_BLOCK
    cs = np.arange(n_cmp) * CMP_STRIDE
    ce = cs + CMP_BLOCK
    ss = np.arange(n_sel) * SEL_BLOCK
    se = ss + SEL_BLOCK
    ov = np.clip(np.minimum(ce[:, None], se[None, :]) - np.maximum(cs[:, None], ss[None, :]), 0, None) / CMP_BLOCK
    out = np.zeros((n_sel, seq // CMP_STRIDE), np.float32)
    out[:, :n_cmp] = ov.T
    return jnp.asarray(out, BF16)


def _block_indicator(seq):
    e = np.zeros((seq, LANES), np.float32)
    e[np.arange(seq), np.arange(seq) // SEL_BLOCK] = 1.0
    return jnp.asarray(e, BF16)


def _gate_expand():
    x = np.zeros((LANES, 3 * NSA_Q), np.float32)
    for col in range(NSA_Q):
        hh = _QA_PERM[col] // HEAD_DIM
        for r in range(3):
            x[hh * 3 + r, r * NSA_Q + col] = 1.0
    return jnp.asarray(x, BF16)


def _neg_later_keys(n):
    u = -(np.arange(n)[:, None] >= np.arange(n)[None, :]).astype(np.float32)
    return jnp.asarray(np.concatenate([u, u], axis=0), BF16)


def _heads_to_qa_order(a, axis):
    shape = a.shape
    a = a.reshape(shape[:axis] + (NSA_KV_HEADS, NSA_GROUP, HEAD_DIM) + shape[axis + 1:])
    return jnp.swapaxes(a, axis, axis + 1).reshape(shape)


def _prep_in_proj(w):
    q_a, kc, vc, ks, vs, kw, vw, g_a, q_b, k_b, v_b, g_m = jnp.split(w.astype(BF16), SPLIT_POINTS, axis=-1)
    g_a = jnp.pad(g_a, ((0, 0), (0, 0), (0, LANES - g_a.shape[-1])))
    cols = [g_m, _heads_to_qa_order(q_a, 2), kc, ks, kw, vc, vs, vw, q_b, k_b, v_b, g_a]
    return jnp.concatenate(cols, axis=-1)


def _prep_compress(pos, w1, b1, w2):
    half = CMP_BLOCK // 2
    pos2 = pos.reshape(2, half, 1, HEAD_DIM)
    pos_t = jnp.broadcast_to(pos2, (2, half, NSA_KV_HEADS, HEAD_DIM)).reshape(2, half * NSA_KV)
    w1r = w1.reshape(2, half, HEAD_DIM, CMP_HIDDEN)
    eye = jnp.eye(NSA_KV_HEADS, dtype=w1.dtype)
    w1x = (w1r[:, :, None, :, None, :] * eye[None, None, :, None, :, None]).reshape(
        2, half * NSA_KV, NSA_KV_HEADS * CMP_HIDDEN)
    b1t = jnp.tile(b1.reshape(1, CMP_HIDDEN), (1, NSA_KV_HEADS))
    w2x = (w2[None, :, None, :] * eye[:, None, :, None]).reshape(NSA_KV_HEADS * CMP_HIDDEN, NSA_KV)
    return pos_t, w1x[0].astype(BF16), w1x[1].astype(BF16), b1t, w2x.astype(BF16)


def kernel(x, w_in, cmp_pos_k, cmp_w1_k, cmp_b1_k, cmp_w2_k, cmp_pos_v, cmp_w1_v, cmp_b1_v, cmp_w2_v,
           w_branch_a, w_branch_b, w_out, ln_mix_g, ln_mix_b, w_up, conv_w, conv_b, w_down,
           ln_ffn_g, ln_ffn_b):
    b, s, d = x.shape
    t = b * s
    assert d == D_MODEL and s // SEL_BLOCK == N_SEL and s // CMP_STRIDE == LANES
    cos_t, sin_prev, sin_next = _rope_tables(s)
    ovt = _selection_overlap_t(s)
    expand = _block_indicator(s)
    gate_x = _gate_expand()
    upper = _neg_later_keys(SB_TILE)

    w_proj = _prep_in_proj(w_in)
    wa = _heads_to_qa_order(w_branch_a.astype(BF16), 1)
    wb = w_branch_b.astype(BF16)
    wo = w_out.astype(BF16)
    wu = w_up.astype(BF16)
    wd = w_down.astype(BF16)
    stack_row = lambda a: a.reshape(a.shape[0], 1, a.shape[1])
    cb, g_mix, b_mix, g_ffn, b_ffn = map(stack_row, (conv_b, ln_mix_g, ln_mix_b, ln_ffn_g, ln_ffn_b))

    xf = x.reshape(t, d)
    xb = xf.astype(BF16)
    for l in range(DEPTH):
        outs = _in_proj(xb, w_proj, l, cos_t, sin_prev, sin_next, s)
        qa, kcmp, ksel, kwin, vcmp, vsel, vwin, qb, kb, vb, ga = outs
        rows = s // CMP_STRIDE
        kc = _compress(kcmp.reshape(b, rows, CMP_STRIDE * LANES),
                       *_prep_compress(cmp_pos_k[l], cmp_w1_k[l], cmp_b1_k[l], cmp_w2_k[l]))
        vc = _compress(vcmp.reshape(b, rows, CMP_STRIDE * LANES),
                       *_prep_compress(cmp_pos_v[l], cmp_w1_v[l], cmp_b1_v[l], cmp_w2_v[l]))
        r3 = lambda a: a.reshape(b, s, a.shape[-1])
        oa = _nsa_attention(r3(qa), kc, vc, r3(ksel), r3(vsel), r3(kwin), r3(vwin), r3(ga),
                            ovt, expand, gate_x)
        ob = _sb_attention(r3(qb), r3(kb), r3(vb), upper)
        xf, xb = _merge(oa.reshape(t, NSA_Q), ob.reshape(t, SB_W), xb, xf, l,
                        w_proj, wa, wb, wo, g_mix, b_mix)
        xf, xb = _conv_ffn(xb, xf, l, wu, conv_w, cb, wd, g_ffn, b_ffn, s)
    return xf.reshape(b, s, d)
```

```python
import functools

import numpy as np
import jax
import jax.numpy as jnp
from jax import lax
from jax.experimental import pallas as pl
from jax.experimental.pallas import tpu as pltpu

D_MODEL = 1024
DEPTH = 4
HEAD_DIM = 64
NSA_HEADS = 8
NSA_KV_HEADS = 2
NSA_GROUP = NSA_HEADS // NSA_KV_HEADS
SB_HEADS = 8
CMP_BLOCK = 32
CMP_STRIDE = 16
CMP_HIDDEN = 128
SEL_BLOCK = 64
SEL_TOPK = 16
WINDOW = 512
ROPE_THETA = 500000.0
ROT_DIM = HEAD_DIM // 4
D_FF = 2816
CONV_W = 3
LN_EPS = 1e-5
NEG = -1e30
FORCE = 1e4
DEEPNORM_ALPHA = (2.0 * DEPTH) ** 0.25

NSA_Q = NSA_HEADS * HEAD_DIM
NSA_KV = NSA_KV_HEADS * HEAD_DIM
SB_W = SB_HEADS * HEAD_DIM
SPLIT_SIZES = (NSA_Q, NSA_KV, NSA_KV, NSA_KV, NSA_KV, NSA_KV, NSA_KV, 3 * NSA_HEADS, SB_W, SB_W, SB_W, 2 * D_MODEL)
SPLIT_POINTS = tuple(int(v) for v in np.cumsum(SPLIT_SIZES)[:-1])

LANES = 128
VMEM_LIMIT = 56 * 1024 * 1024

BF16 = jnp.bfloat16
F32 = jnp.float32

Q_TILE = 128
SEL_KEY_TILE = 512
SB_TILE = 256
N_SEL = 32
SEL_CHAINS = 2


def _dot(a, b):
    return jnp.dot(a, b, preferred_element_type=F32)


def _dot_nt(a, b):
    return lax.dot_general(a, b, (((1,), (1,)), ((), ())), preferred_element_type=F32)


def _split_bf16(x):
    hi = x.astype(BF16)
    lo = (x - hi.astype(F32)).astype(BF16)
    return hi, lo


def _layer_norm(z, g, b):
    mu = jnp.mean(z, axis=-1, keepdims=True)
    zc = z - mu
    var = jnp.mean(zc * zc, axis=-1, keepdims=True)
    return zc * lax.rsqrt(var + LN_EPS) * g + b


def _resident(a):
    return pl.BlockSpec(a.shape, lambda *_: (0,) * a.ndim, pipeline_mode=pl.Buffered(1))


def _layer_block(a, l):
    return pl.BlockSpec((None,) + a.shape[1:], lambda *_: (l,) + (0,) * (a.ndim - 1), pipeline_mode=pl.Buffered(1))


LOG2E = 1.4426950408889634
Q_SCALE = HEAD_DIM ** -0.5 * LOG2E
_PROJ_OUT = (
    ("qa", NSA_Q, BF16, True, False, Q_SCALE),
    ("kcmp", LANES, F32, True, False, 1.0),
    ("ksel", LANES, BF16, True, False, 1.0),
    ("kwin", LANES, BF16, True, False, 1.0),
    ("vcmp", LANES, F32, False, False, 1.0),
    ("vsel", LANES, BF16, False, False, 1.0),
    ("vwin", LANES, BF16, False, False, 1.0),
    ("qb", SB_W, BF16, False, False, Q_SCALE),
    ("kb", SB_W, BF16, False, False, 1.0),
    ("vb", SB_W, BF16, False, False, 1.0),
    ("ga", LANES, F32, False, True, 1.0),
)
_PROJ_WIDTH = sum(o[1] for o in _PROJ_OUT)
_ROWS16 = ("kcmp", "vcmp")
_GATE_WIDTH = 2 * D_MODEL
PROJ_TM = 512


def _rope(acc, cos_t, sin_prev, sin_next):
    chunks = []
    for c in range(acc.shape[1] // LANES):
        xc = acc[:, c * LANES:(c + 1) * LANES]
        chunks.append(xc * cos_t + pltpu.roll(xc, ROT_DIM // 2, 1) * sin_prev
                      + pltpu.roll(xc, LANES - ROT_DIM // 2, 1) * sin_next)
    return chunks[0] if len(chunks) == 1 else jnp.concatenate(chunks, axis=1)


def _proj_kernel(x_ref, w_ref, cos_ref, sp_ref, sn_ref, *refs):
    out_refs, stage_ref = refs[:-1], refs[-1]
    x = x_ref[...]
    tm = x.shape[0]

    def finish(acc, spec, o_ref):
        name, _, dtype, rotary, sigmoid, scale = spec
        if rotary:
            acc = _rope(acc, cos_ref[...], sp_ref[...], sn_ref[...])
        if sigmoid:
            acc = jax.nn.sigmoid(acc)
        if scale != 1.0:
            acc = acc * scale
        if name in _ROWS16:
            stage_ref[...] = acc
            for l in range(CMP_STRIDE):
                o_ref[:, l * LANES:(l + 1) * LANES] = stage_ref[pl.ds(l, tm // CMP_STRIDE, stride=CMP_STRIDE), :]
        else:
            o_ref[...] = acc.astype(dtype)

    col = _GATE_WIDTH
    pending = None
    for spec, o_ref in zip(_PROJ_OUT, out_refs):
        acc = _dot(x, w_ref[:, col:col + spec[1]])
        if pending is not None:
            finish(*pending)
        pending = (acc, spec, o_ref)
        col += spec[1]
    finish(*pending)


def _in_proj(x_bf, w, layer, cos_t, sin_prev, sin_next, seq):
    t = x_bf.shape[0]
    tm = PROJ_TM
    tiles_per_seq = seq // tm
    tab_spec = pl.BlockSpec((tm, LANES), lambda i: (i % tiles_per_seq, 0))
    fold = lambda o: CMP_STRIDE if o[0] in _ROWS16 else 1
    return pl.pallas_call(
        _proj_kernel,
        grid=(t // tm,),
        in_specs=[
            pl.BlockSpec((tm, D_MODEL), lambda i: (i, 0)),
            _layer_block(w, layer),
            tab_spec, tab_spec, tab_spec,
        ],
        out_specs=[pl.BlockSpec((tm // fold(o), o[1] * fold(o)), lambda i: (i, 0)) for o in _PROJ_OUT],
        out_shape=[jax.ShapeDtypeStruct((t // fold(o), o[1] * fold(o)), o[2]) for o in _PROJ_OUT],
        scratch_shapes=[pltpu.VMEM((tm, LANES), F32)],
        compiler_params=pltpu.CompilerParams(
            dimension_semantics=("parallel",), vmem_limit_bytes=VMEM_LIMIT),
        name="in_proj",
    )(x_bf, w, cos_t, sin_prev, sin_next)


def _gelu_tanh(x):
    return 0.5 * x * (1.0 + jnp.tanh(np.sqrt(2.0 / np.pi).astype(np.float32) * (x + 0.044715 * (x * x * x))))


def _compress_kernel(r_ref, pos_ref, w1_ref, b1_ref, w2_ref, o_ref):
    r = r_ref[0]
    ra = (r + pos_ref[0:1, :]).astype(BF16)
    rb = (r + pos_ref[1:2, :]).astype(BF16)
    p1 = _dot(ra, w1_ref[0])
    p2 = _dot(rb, w1_ref[1])
    nrow = p2.shape[0]
    hid = _gelu_tanh(p1 + pltpu.roll(p2, nrow - 1, 0) + b1_ref[...])
    o_ref[0] = _dot(hid.astype(BF16), w2_ref[...]).astype(o_ref.dtype)


def _compress(r, which, layer, pos, w1, b1, w2):
    b, nrow, width = r.shape
    full = lambda a: pl.BlockSpec((None, None) + a.shape[2:], lambda i: (which, layer) + (0,) * (a.ndim - 2),
                                  pipeline_mode=pl.Buffered(1))
    return pl.pallas_call(
        _compress_kernel,
        grid=(b,),
        in_specs=[pl.BlockSpec((1, nrow, width), lambda i: (i, 0, 0)),
                  full(pos), full(w1), full(b1), full(w2)],
        out_specs=pl.BlockSpec((1, nrow, LANES), lambda i: (i, 0, 0)),
        out_shape=jax.ShapeDtypeStruct((b, nrow, LANES), BF16),
        compiler_params=pltpu.CompilerParams(
            dimension_semantics=("parallel",), vmem_limit_bytes=VMEM_LIMIT),
        name="compress",
    )(r, pos, w1, b1, w2)


def _masked_softmax(s, mask):
    s = jnp.where(mask, s, NEG)
    m = jnp.max(s, axis=-1, keepdims=True)
    e = jnp.where(mask, jnp.exp2(s - m), 0.0)
    return e / jnp.maximum(jnp.sum(e, axis=-1, keepdims=True), 1e-30)


def _tile_rows(a, n):
    return jnp.concatenate([a] * n, axis=0)


def _nsa_kernel(q_ref, kc_ref, vc_ref, ks_ref, vs_ref, kw_ref, vw_ref, ga_ref,
                ovt_ref, et_ref, x_ref, o_ref, osel_ref):
    qi = pl.program_id(1)
    t0 = qi * Q_TILE
    hq = NSA_GROUP * Q_TILE
    nrows = NSA_HEADS * Q_TILE
    lane = lax.broadcasted_iota(jnp.int32, (1, LANES), 1)
    t_col = t0 + lax.broadcasted_iota(jnp.int32, (Q_TILE, 1), 0)
    t_rows = _tile_rows(t_col, NSA_HEADS)
    q_all = q_ref[0]

    gates = _dot(jnp.concatenate(_split_bf16(ga_ref[0]), axis=1), x_ref[...])

    j_col = lax.broadcasted_iota(jnp.int32, (N_SEL, 1), 0)
    t_lane = t0 + lax.broadcasted_iota(jnp.int32, (1, Q_TILE), 1)
    cur = t_lane // SEL_BLOCK
    forced = (j_col == 0) | (j_col == cur) | (j_col == cur - 1)
    valid = j_col * SEL_BLOCK <= t_lane

    w0 = pl.multiple_of(jnp.maximum(t0 - WINDOW, 0), Q_TILE)
    wlen = WINDOW + Q_TILE
    diff = t_col - (w0 + lax.broadcasted_iota(jnp.int32, (1, wlen), 1))
    win_bias = _tile_rows(jnp.where((diff >= 0) & (diff < WINDOW), 0.0, NEG), NSA_HEADS)
    kw = kw_ref[0, pl.ds(w0, wlen), :]
    vw = vw_ref[0, pl.ds(w0, wlen), :]

    last = qi // (SEL_KEY_TILE // Q_TILE)

    qs = jnp.concatenate(
        [jnp.where((lane // HEAD_DIM) == g, q_all[:, h * LANES:(h + 1) * LANES], jnp.zeros((), BF16))
         for g in range(NSA_KV_HEADS) for h in range(NSA_GROUP)], axis=0)

    s = _dot_nt(qs, kc_ref[0])
    m_cmp = (CMP_STRIDE * lane + CMP_BLOCK - 1) <= t_rows
    p = _masked_softmax(s, m_cmp)
    o_cmp = _dot(p.astype(BF16), vc_ref[0])

    sw = _dot_nt(qs, kw) + win_bias
    ew = jnp.exp2(sw - jnp.max(sw, axis=-1, keepdims=True)).astype(BF16)
    ow = _dot(ew, jnp.concatenate([vw, jnp.ones((wlen, LANES), BF16)], axis=1))
    o_win = ow[:, :LANES] / jnp.maximum(ow[:, LANES:], 1e-30)

    drops = []
    for g in range(NSA_KV_HEADS):
        pg = p[g * hq:(g + 1) * hq]
        psum = pg[0:Q_TILE] + pg[Q_TILE:2 * Q_TILE] + pg[2 * Q_TILE:3 * Q_TILE] + pg[3 * Q_TILE:4 * Q_TILE]
        ps_hi, ps_lo = _split_bf16(psum)
        score = _dot_nt(ovt_ref[...], ps_hi) + _dot_nt(ovt_ref[...], ps_lo)
        score = jnp.where(forced, FORCE, jnp.where(valid, score, -FORCE))
        rank = jnp.zeros((N_SEL, Q_TILE), F32)
        for jp in range(N_SEL):
            row = score[jp:jp + 1, :]
            ge = jnp.where(row >= score, 1.0, 0.0)
            gt = jnp.where(row > score, 1.0, 0.0)
            rank = rank + jnp.where(j_col > jp, ge, gt)
        drop_t = jnp.where(rank < SEL_TOPK, 0.0, NEG)
        drop_t = jnp.concatenate([drop_t, jnp.zeros((LANES - N_SEL, Q_TILE), F32)], axis=0)
        drop = drop_t.T.astype(BF16)
        drops.append(_tile_rows(drop, NSA_GROUP))
    q_aug = jnp.concatenate([qs, jnp.concatenate(drops, axis=0)], axis=1)

    crow = nrows // SEL_CHAINS
    q_parts = [q_aug[c * crow:(c + 1) * crow] for c in range(SEL_CHAINS)]

    def sel_tile(k0, carry, mask):
        k_aug = jnp.concatenate([ks_ref[0, pl.ds(k0, SEL_KEY_TILE), :],
                                 et_ref[pl.ds(k0, SEL_KEY_TILE), :]], axis=1)
        v_aug = jnp.concatenate([vs_ref[0, pl.ds(k0, SEL_KEY_TILE), :],
                                 jnp.ones((SEL_KEY_TILE, LANES), BF16)], axis=1)
        sc = [_dot_nt(q, k_aug) for q in q_parts]
        if mask is not None:
            sc = [jnp.where(mask, s_c, NEG) for s_c in sc]
        m_new = [jnp.maximum(m_i, jnp.max(s_c, axis=-1, keepdims=True)) for s_c, (m_i, _) in zip(sc, carry)]
        e = [jnp.exp2(s_c - m_c).astype(BF16) for s_c, m_c in zip(sc, m_new)]
        return tuple((m_c, jnp.exp2(m_i - m_c) * acc + _dot(e_c, v_aug))
                     for m_c, e_c, (m_i, acc) in zip(m_new, e, carry))

    for n_tiles in range(1, et_ref.shape[0] // SEL_KEY_TILE + 1):
        @pl.when(last == n_tiles - 1)
        def _():
            carry = ((jnp.full((crow, 1), NEG, F32), jnp.zeros((crow, 2 * LANES), F32)),) * SEL_CHAINS
            for kt in range(n_tiles - 1):
                carry = sel_tile(kt * SEL_KEY_TILE, carry, None)
            k_diag = (n_tiles - 1) * SEL_KEY_TILE
            causal = _tile_rows((k_diag + lax.broadcasted_iota(jnp.int32, (1, SEL_KEY_TILE), 1)) <= t_col,
                                crow // Q_TILE)
            acc = jnp.concatenate([a for _, a in sel_tile(k_diag, carry, causal)], axis=0)
            osel_ref[...] = acc[:, :LANES] / jnp.maximum(acc[:, LANES:], 1e-30)
    o_sel = osel_ref[...]

    branches = (o_cmp, o_sel, o_win)
    for h in range(NSA_GROUP):
        mixed = []
        for g in range(NSA_KV_HEADS):
            r0 = (g * NSA_GROUP + h) * Q_TILE
            acc = None
            for r in range(3):
                term = gates[:, r * NSA_Q + h * LANES: r * NSA_Q + (h + 1) * LANES] * branches[r][r0:r0 + Q_TILE]
                acc = term if acc is None else acc + term
            mixed.append(acc)
        o_ref[0, :, h * LANES:(h + 1) * LANES] = jnp.where(lane < HEAD_DIM, mixed[0], mixed[1]).astype(o_ref.dtype)


def _nsa_attention(qa, kc, vc, ksel, vsel, kwin, vwin, ga, ovt, expand, gate_x):
    b, s, _ = qa.shape
    per_q = lambda w: pl.BlockSpec((1, Q_TILE, w), lambda i, j: (i, j, 0))
    per_b = lambda a: pl.BlockSpec((1,) + a.shape[1:], lambda i, j: (i, 0, 0))
    full = _resident
    return pl.pallas_call(
        _nsa_kernel,
        grid=(b, s // Q_TILE),
        in_specs=[per_q(NSA_Q), per_b(kc), per_b(vc), per_b(ksel), per_b(vsel), per_b(kwin), per_b(vwin),
                  per_q(LANES), full(ovt), full(expand), full(gate_x)],
        out_specs=per_q(NSA_Q),
        out_shape=jax.ShapeDtypeStruct((b, s, NSA_Q), BF16),
        scratch_shapes=[pltpu.VMEM((NSA_HEADS * Q_TILE, LANES), F32)],
        compiler_params=pltpu.CompilerParams(
            dimension_semantics=("parallel", "parallel"), vmem_limit_bytes=VMEM_LIMIT),
        name="nsa_attention",
    )(qa, kc, vc, ksel, vsel, kwin, vwin, ga, ovt, expand, gate_x)


SB_SKIP = 150.0
SB_PAIRS = 4


def _sb_kernel(q_ref, k_ref, v_ref, u_ref, o_ref, run_ref, acc_ref):
    qi = pl.program_id(2)
    t0 = qi * SB_TILE
    lane = lax.broadcasted_iota(jnp.int32, (1, LANES), 1)
    t_col = t0 + lax.broadcasted_iota(jnp.int32, (SB_TILE, 1), 0)
    diag_mask = (t0 + lax.broadcasted_iota(jnp.int32, (1, SB_TILE), 1)) < t_col
    diag_mask = _tile_rows(diag_mask, 2)
    qms = []
    for p in range(SB_PAIRS):
        q = q_ref[0, :, p * LANES:(p + 1) * LANES]
        qms.append(jnp.concatenate(
            [jnp.where((lane // HEAD_DIM) == hl, q, jnp.zeros((), BF16)) for hl in range(2)], axis=0))

    def tiles(k0, state, mask):
        pairs = range(SB_PAIRS)
        cols = [slice(p * LANES, (p + 1) * LANES) for p in pairs]
        z2 = [_dot_nt(qms[p], k_ref[0, pl.ds(k0, SB_TILE), cols[p]]) for p in pairs]
        split = []
        for p in pairs:
            neg_abs = pltpu.bitcast(pltpu.bitcast(z2[p], jnp.uint32) | jnp.uint32(0x80000000), F32)
            nfail = jnp.maximum(z2[p], 0.0) + jnp.log2(1.0 + jnp.exp2(neg_abs))
            if mask is not None:
                nfail = jnp.where(mask, nfail, 0.0)
            split.append(jnp.concatenate(_split_bf16(nfail), axis=1))
        tail = [_dot(split[p], u_ref[...]) for p in pairs]
        a = []
        for p in pairs:
            ap = jnp.exp2(z2[p] + tail[p] + state[p][0])
            if mask is not None:
                ap = jnp.where(mask, ap, 0.0)
            a.append(ap.astype(BF16))
        return tuple((state[p][0] + tail[p][:, 0:1],
                      state[p][1] + _dot(a[p], v_ref[0, pl.ds(k0, SB_TILE), cols[p]])) for p in pairs)

    zero = (jnp.zeros((2 * SB_TILE, 1), F32), jnp.zeros((2 * SB_TILE, LANES), F32))

    def save(state):
        for p, (run, acc) in enumerate(state):
            run_ref[p] = run
            acc_ref[p] = acc

    @pl.when(qi == 0)
    def _():
        save(tiles(0, (zero,) * SB_PAIRS, diag_mask))

    @pl.when(qi > 0)
    def _():
        state = tiles(pl.multiple_of(t0, SB_TILE), (zero,) * SB_PAIRS, diag_mask)
        save(tiles(pl.multiple_of(t0 - SB_TILE, SB_TILE), state, None))

    state = tuple((run_ref[p], acc_ref[p]) for p in range(SB_PAIRS))

    def cond(c):
        i, state = c
        top = functools.reduce(jnp.maximum, [run for run, _ in state])
        return jnp.logical_and(i <= qi, jnp.max(top) > -SB_SKIP)

    def body(c):
        i, state = c
        k0 = pl.multiple_of((qi - i) * SB_TILE, SB_TILE)
        return i + 1, tiles(k0, state, None)

    _, state = lax.while_loop(cond, body, (jnp.int32(2), state))
    for p, (_, acc) in enumerate(state):
        o_ref[0, :, p * LANES:(p + 1) * LANES] = jnp.where(
            lane < HEAD_DIM, acc[:SB_TILE], acc[SB_TILE:]).astype(o_ref.dtype)


def _sb_attention(qb, kb, vb, upper):
    b, s, w = qb.shape
    wb = SB_PAIRS * LANES
    return pl.pallas_call(
        _sb_kernel,
        grid=(b, w // wb, s // SB_TILE),
        in_specs=[pl.BlockSpec((1, SB_TILE, wb), lambda i, j, t: (i, t, j)),
                  pl.BlockSpec((1, s, wb), lambda i, j, t: (i, 0, j)),
                  pl.BlockSpec((1, s, wb), lambda i, j, t: (i, 0, j)),
                  _resident(upper)],
        out_specs=pl.BlockSpec((1, SB_TILE, wb), lambda i, j, t: (i, t, j)),
        out_shape=jax.ShapeDtypeStruct((b, s, w), BF16),
        scratch_shapes=[pltpu.VMEM((SB_PAIRS, 2 * SB_TILE, 1), F32), pltpu.VMEM((SB_PAIRS, 2 * SB_TILE, LANES), F32)],
        compiler_params=pltpu.CompilerParams(
            dimension_semantics=("parallel", "parallel", "parallel"), vmem_limit_bytes=VMEM_LIMIT),
        name="sb_attention",
    )(qb, kb, vb, upper)


MERGE_TM = 512
MERGE_CHAINS = 2


def _merge_kernel(oa_ref, ob_ref, xb_ref, x_ref, wg_ref, wa_ref, wb_ref, wo_ref, g_ref, b_ref, y_ref, ybf_ref):
    half = xb_ref.shape[0] // MERGE_CHAINS
    rows = [slice(c * half, (c + 1) * half) for c in range(MERGE_CHAINS)]
    ya = [_dot(oa_ref[r, :], wa_ref[...]) for r in rows]
    ga = [_dot(xb_ref[r, :], wg_ref[:, :D_MODEL]) for r in rows]
    merged = [jax.nn.sigmoid(g_c) * y_c for g_c, y_c in zip(ga, ya)]
    yb = [_dot(ob_ref[r, :], wb_ref[...]) for r in rows]
    gb = [_dot(xb_ref[r, :], wg_ref[:, D_MODEL:]) for r in rows]
    merged = [m_c + jax.nn.sigmoid(g_c) * y_c for m_c, g_c, y_c in zip(merged, gb, yb)]
    y = [_dot(m_c.astype(BF16), wo_ref[...]) for m_c in merged]
    for r, y_c in zip(rows, y):
        out = _layer_norm(DEEPNORM_ALPHA * x_ref[r, :] + y_c, g_ref[...], b_ref[...])
        y_ref[r, :] = out
        ybf_ref[r, :] = out.astype(BF16)


def _merge(oa, ob, x_bf, x, layer, wg, wa, wb, wo, g, b):
    t = x.shape[0]
    tm = MERGE_TM
    row = lambda w: pl.BlockSpec((tm, w), lambda i: (i, 0))
    full = lambda a: _layer_block(a, layer)
    gate_cols = pl.BlockSpec((None, D_MODEL, _GATE_WIDTH), lambda i: (layer, 0, 0), pipeline_mode=pl.Buffered(1))
    return pl.pallas_call(
        _merge_kernel,
        grid=(t // tm,),
        in_specs=[row(NSA_Q), row(SB_W), row(D_MODEL), row(D_MODEL),
                  gate_cols, full(wa), full(wb), full(wo), full(g), full(b)],
        out_specs=[row(D_MODEL), row(D_MODEL)],
        out_shape=[jax.ShapeDtypeStruct((t, D_MODEL), F32), jax.ShapeDtypeStruct((t, D_MODEL), BF16)],
        compiler_params=pltpu.CompilerParams(
            dimension_semantics=("parallel",), vmem_limit_bytes=VMEM_LIMIT),
        name="merge_out_ln",
    )(oa, ob, x_bf, x, wg, wa, wb, wo, g, b)


FFN_TM = 512
FFN_CHUNK = 256
HALO = 16


def _ffn_kernel(h_ref, halo_ref, x_ref, wu_ref, cw_ref, cb_ref, wd_ref, g_ref, b_ref, y_ref, ybf_ref,
                hx_ref, buf_ref, act_ref, *, tiles_per_seq):
    i = pl.program_id(0)
    has_prev = (i % tiles_per_seq) != 0
    tm = h_ref.shape[0]
    n_chunks = D_FF // FFN_CHUNK
    hx_ref[0:HALO, :] = jnp.where(has_prev, halo_ref[...], jnp.zeros((), BF16))
    hx_ref[HALO:, :] = h_ref[...]

    def up(c):
        for half in range(2):
            cols = slice(half * D_FF + c * FFN_CHUNK, half * D_FF + (c + 1) * FFN_CHUNK)
            buf_ref[half, c % 2] = _dot(hx_ref[...], wu_ref[:, cols])

    def gate(c):
        def conv(half):
            cols = slice(half * D_FF + c * FFN_CHUNK, half * D_FF + (c + 1) * FFN_CHUNK)
            u = buf_ref.at[half, c % 2]
            return (cw_ref[0:1, cols] * u[HALO - 2:HALO - 2 + tm, :] + cw_ref[1:2, cols] * u[HALO - 1:HALO - 1 + tm, :]
                    + cw_ref[2:3, cols] * u[HALO:HALO + tm, :] + cb_ref[:, cols])
        a = conv(0)
        act_ref[:, c * FFN_CHUNK:(c + 1) * FFN_CHUNK] = (a * jax.nn.sigmoid(a) * conv(1)).astype(BF16)

    up(0)
    for c in range(1, n_chunks):
        up(c)
        gate(c - 1)
    gate(n_chunks - 1)
    y = _dot(act_ref[...], wd_ref[...])
    out = _layer_norm(DEEPNORM_ALPHA * x_ref[...] + y, g_ref[...], b_ref[...])
    y_ref[...] = out
    ybf_ref[...] = out.astype(BF16)


def _conv_ffn(h_bf, x, layer, w_up, conv_w, conv_b, w_down, g, b, seq):
    t = x.shape[0]
    tm = FFN_TM
    row = lambda w: pl.BlockSpec((tm, w), lambda i: (i, 0))
    return pl.pallas_call(
        functools.partial(_ffn_kernel, tiles_per_seq=seq // tm),
        grid=(t // tm,),
        in_specs=[row(D_MODEL),
                  pl.BlockSpec((HALO, D_MODEL), lambda i: (jnp.maximum(i * (tm // HALO) - 1, 0), 0)),
                  row(D_MODEL),
                  *[_layer_block(a, layer) for a in (w_up, conv_w, conv_b, w_down, g, b)]],
        out_specs=[row(D_MODEL), row(D_MODEL)],
        out_shape=[jax.ShapeDtypeStruct((t, D_MODEL), F32), jax.ShapeDtypeStruct((t, D_MODEL), BF16)],
        scratch_shapes=[pltpu.VMEM((tm + HALO, D_MODEL), BF16),
                        pltpu.VMEM((2, 2, tm + HALO, FFN_CHUNK), F32), pltpu.VMEM((tm, D_FF), BF16)],
        compiler_params=pltpu.CompilerParams(
            dimension_semantics=("parallel",), vmem_limit_bytes=VMEM_LIMIT),
        name="conv_ffn_ln",
    )(h_bf, h_bf, x, w_up, conv_w, conv_b, w_down, g, b)


def _rope_tables(seq):
    inv_freq = ROPE_THETA ** (-np.arange(0, ROT_DIM, 2, dtype=np.float32) / ROT_DIM)
    ang = jnp.arange(seq, dtype=F32)[:, None] * jnp.asarray(inv_freq, F32)[None, :]
    cos, sin = jnp.cos(ang), jnp.sin(ang)
    half = ROT_DIM // 2
    ones = jnp.ones((seq, HEAD_DIM - ROT_DIM), F32)
    zeros = jnp.zeros((seq, HEAD_DIM - ROT_DIM), F32)
    zh = jnp.zeros((seq, half), F32)
    cos_t = jnp.concatenate([cos, cos, ones], axis=1)
    sin_prev = jnp.concatenate([zh, sin, zeros], axis=1)
    sin_next = jnp.concatenate([-sin, zh, zeros], axis=1)
    rep = LANES // HEAD_DIM
    return tuple(jnp.tile(a, (1, rep)) for a in (cos_t, sin_prev, sin_next))


_QA_PERM = np.concatenate([np.concatenate([np.arange(h * HEAD_DIM, (h + 1) * HEAD_DIM),
                                           np.arange((NSA_GROUP + h) * HEAD_DIM, (NSA_GROUP + h + 1) * HEAD_DIM)])
                           for h in range(NSA_GROUP)])


def _selection_overlap_t(seq):
    n_cmp = (seq - CMP_BLOCK) // CMP_STRIDE + 1
    n_sel = seq // SEL_BLOCK
    cs = np.arange(n_cmp) * CMP_STRIDE
    ce = cs + CMP_BLOCK
    ss = np.arange(n_sel) * SEL_BLOCK
    se = ss + SEL_BLOCK
    ov = np.clip(np.minimum(ce[:, None], se[None, :]) - np.maximum(cs[:, None], ss[None, :]), 0, None) / CMP_BLOCK
    out = np.zeros((n_sel, seq // CMP_STRIDE), np.float32)
    out[:, :n_cmp] = ov.T
    return jnp.asarray(out, BF16)


def _block_indicator(seq):
    e = np.zeros((seq, LANES), np.float32)
    e[np.arange(seq), np.arange(seq) // SEL_BLOCK] = 1.0
    return jnp.asarray(e, BF16)


def _gate_expand():
    x = np.zeros((LANES, 3 * NSA_Q), np.float32)
    for col in range(NSA_Q):
        hh = _QA_PERM[col] // HEAD_DIM
        for r in range(3):
            x[hh * 3 + r, r * NSA_Q + col] = 1.0
    return jnp.asarray(np.concatenate([x, x], axis=0), BF16)


def _neg_later_keys(n):
    u = -(np.arange(n)[:, None] >= np.arange(n)[None, :]).astype(np.float32)
    return jnp.asarray(np.concatenate([u, u], axis=0), BF16)


def _heads_to_qa_order(a, axis):
    shape = a.shape
    a = a.reshape(shape[:axis] + (NSA_KV_HEADS, NSA_GROUP, HEAD_DIM) + shape[axis + 1:])
    return jnp.swapaxes(a, axis, axis + 1).reshape(shape)


def _prep_in_proj(w):
    q_a, kc, vc, ks, vs, kw, vw, g_a, q_b, k_b, v_b, g_m = jnp.split(w.astype(BF16), SPLIT_POINTS, axis=-1)
    g_a = jnp.pad(g_a, ((0, 0), (0, 0), (0, LANES - g_a.shape[-1])))
    cols = [g_m, _heads_to_qa_order(q_a, 2), kc, ks, kw, vc, vs, vw, q_b, k_b, v_b, g_a]
    return jnp.concatenate(cols, axis=-1)


def _prep_compress(pos, w1, b1, w2):
    half = CMP_BLOCK // 2
    lead = pos.shape[:2]
    pos_t = jnp.broadcast_to(pos.reshape(lead + (2, half, 1, HEAD_DIM)),
                             lead + (2, half, NSA_KV_HEADS, HEAD_DIM)).reshape(lead + (2, half * NSA_KV))
    eye = jnp.eye(NSA_KV_HEADS, dtype=w1.dtype)
    w1r = w1.reshape(lead + (2, half, 1, HEAD_DIM, 1, CMP_HIDDEN))
    w1x = (w1r * eye[:, None, :, None]).reshape(lead + (2, half * NSA_KV, NSA_KV_HEADS * CMP_HIDDEN))
    b1t = jnp.tile(b1.reshape(lead + (1, CMP_HIDDEN)), (1, 1, 1, NSA_KV_HEADS))
    w2x = (w2.reshape(lead + (1, CMP_HIDDEN, 1, HEAD_DIM)) * eye[:, None, :, None]).reshape(
        lead + (NSA_KV_HEADS * CMP_HIDDEN, NSA_KV))
    return pos_t, w1x.astype(BF16), b1t, w2x.astype(BF16)


def kernel(x, w_in, cmp_pos_k, cmp_w1_k, cmp_b1_k, cmp_w2_k, cmp_pos_v, cmp_w1_v, cmp_b1_v, cmp_w2_v,
           w_branch_a, w_branch_b, w_out, ln_mix_g, ln_mix_b, w_up, conv_w, conv_b, w_down,
           ln_ffn_g, ln_ffn_b):
    b, s, d = x.shape
    t = b * s
    assert d == D_MODEL and s // SEL_BLOCK == N_SEL and s // CMP_STRIDE == LANES
    cos_t, sin_prev, sin_next = _rope_tables(s)
    ovt = _selection_overlap_t(s)
    expand = _block_indicator(s)
    gate_x = _gate_expand()
    upper = _neg_later_keys(SB_TILE)

    w_proj = _prep_in_proj(w_in)
    wa = _heads_to_qa_order(w_branch_a.astype(BF16), 1)
    wb = w_branch_b.astype(BF16)
    wo = w_out.astype(BF16)
    wu = w_up.astype(BF16)
    wd = w_down.astype(BF16)
    cmp_params = _prep_compress(*(jnp.stack(kv) for kv in ((cmp_pos_k, cmp_pos_v), (cmp_w1_k, cmp_w1_v),
                                                           (cmp_b1_k, cmp_b1_v), (cmp_w2_k, cmp_w2_v))))
    stack_row = lambda a: a.reshape(a.shape[0], 1, a.shape[1])
    cb, g_mix, b_mix, g_ffn, b_ffn = map(stack_row, (conv_b, ln_mix_g, ln_mix_b, ln_ffn_g, ln_ffn_b))

    xf = x.reshape(t, d)
    xb = xf.astype(BF16)
    for l in range(DEPTH):
        outs = _in_proj(xb, w_proj, l, cos_t, sin_prev, sin_next, s)
        qa, kcmp, ksel, kwin, vcmp, vsel, vwin, qb, kb, vb, ga = outs
        rows = s // CMP_STRIDE
        kc = _compress(kcmp.reshape(b, rows, CMP_STRIDE * LANES), 0, l, *cmp_params)
        vc = _compress(vcmp.reshape(b, rows, CMP_STRIDE * LANES), 1, l, *cmp_params)
        r3 = lambda a: a.reshape(b, s, a.shape[-1])
        oa = _nsa_attention(r3(qa), kc, vc, r3(ksel), r3(vsel), r3(kwin), r3(vwin), r3(ga),
                            ovt, expand, gate_x)
        ob = _sb_attention(r3(qb), r3(kb), r3(vb), upper)
        xf, xb = _merge(oa.reshape(t, NSA_Q), ob.reshape(t, SB_W), xb, xf, l,
                        w_proj, wa, wb, wo, g_mix, b_mix)
        xf, xb = _conv_ffn(xb, xf, l, wu, conv_w, cb, wd, g_ffn, b_ffn, s)
    return xf.reshape(b, s, d)
```

```python
import functools

import numpy as np
import jax
import jax.numpy as jnp
from jax import lax
from jax.experimental import pallas as pl
from jax.experimental.pallas import tpu as pltpu

D_MODEL = 1024
DEPTH = 4
HEAD_DIM = 64
NSA_HEADS = 8
NSA_KV_HEADS = 2
NSA_GROUP = NSA_HEADS // NSA_KV_HEADS
SB_HEADS = 8
CMP_BLOCK = 32
CMP_STRIDE = 16
CMP_HIDDEN = 128
SEL_BLOCK = 64
SEL_TOPK = 16
WINDOW = 512
ROPE_THETA = 500000.0
ROT_DIM = HEAD_DIM // 4
D_FF = 2816
CONV_W = 3
LN_EPS = 1e-5
NEG = -1e30
FORCE = 1e4
DEEPNORM_ALPHA = (2.0 * DEPTH) ** 0.25

NSA_Q = NSA_HEADS * HEAD_DIM
NSA_KV = NSA_KV_HEADS * HEAD_DIM
SB_W = SB_HEADS * HEAD_DIM
SPLIT_SIZES = (NSA_Q, NSA_KV, NSA_KV, NSA_KV, NSA_KV, NSA_KV, NSA_KV, 3 * NSA_HEADS, SB_W, SB_W, SB_W, 2 * D_MODEL)
SPLIT_POINTS = tuple(int(v) for v in np.cumsum(SPLIT_SIZES)[:-1])

LANES = 128
VMEM_LIMIT = 56 * 1024 * 1024

BF16 = jnp.bfloat16
F32 = jnp.float32

Q_TILE = 128
SEL_KEY_TILE = 512
SB_TILE = 256
N_SEL = 32
SEL_CHAINS = 2


def _dot(a, b):
    return jnp.dot(a, b, preferred_element_type=F32)


def _dot_nt(a, b):
    return lax.dot_general(a, b, (((1,), (1,)), ((), ())), preferred_element_type=F32)


def _split_bf16(x):
    hi = x.astype(BF16)
    lo = (x - hi.astype(F32)).astype(BF16)
    return hi, lo


def _layer_norm(z, g, b):
    mu = jnp.mean(z, axis=-1, keepdims=True)
    zc = z - mu
    var = jnp.mean(zc * zc, axis=-1, keepdims=True)
    return zc * lax.rsqrt(var + LN_EPS) * g + b


def _resident(a):
    return pl.BlockSpec(a.shape, lambda *_: (0,) * a.ndim, pipeline_mode=pl.Buffered(1))


def _layer_block(a, l):
    return pl.BlockSpec((None,) + a.shape[1:], lambda *_: (l,) + (0,) * (a.ndim - 1), pipeline_mode=pl.Buffered(1))


LOG2E = 1.4426950408889634
Q_SCALE = HEAD_DIM ** -0.5 * LOG2E
_PROJ_OUT = (
    ("qa", NSA_Q, BF16, True, False, Q_SCALE),
    ("kcmp", LANES, F32, True, False, 1.0),
    ("ksel", LANES, BF16, True, False, 1.0),
    ("kwin", LANES, BF16, True, False, 1.0),
    ("vcmp", LANES, F32, False, False, 1.0),
    ("vsel", LANES, BF16, False, False, 1.0),
    ("vwin", LANES, BF16, False, False, 1.0),
    ("qb", SB_W, BF16, False, False, Q_SCALE),
    ("kb", SB_W, BF16, False, False, 1.0),
    ("vb", SB_W, BF16, False, False, 1.0),
    ("ga", LANES, F32, False, True, 1.0),
)
_PROJ_WIDTH = sum(o[1] for o in _PROJ_OUT)
_PROJ_RUNS = (7, 4)
_ROWS16 = ("kcmp", "vcmp")
_GATE_WIDTH = 2 * D_MODEL
PROJ_TM = 512


def _rope(acc, cos_t, sin_prev, sin_next):
    chunks = []
    for c in range(acc.shape[1] // LANES):
        xc = acc[:, c * LANES:(c + 1) * LANES]
        chunks.append(xc * cos_t + pltpu.roll(xc, ROT_DIM // 2, 1) * sin_prev
                      + pltpu.roll(xc, LANES - ROT_DIM // 2, 1) * sin_next)
    return chunks[0] if len(chunks) == 1 else jnp.concatenate(chunks, axis=1)


def _proj_kernel(x_ref, w_ref, cos_ref, sp_ref, sn_ref, *refs):
    out_refs, stage_ref = refs[:-1], refs[-1]
    x = x_ref[...]
    tm = x.shape[0]

    def finish(acc, spec, o_ref):
        name, _, dtype, rotary, sigmoid, scale = spec
        if rotary:
            acc = _rope(acc, cos_ref[...], sp_ref[...], sn_ref[...])
        if sigmoid:
            acc = jax.nn.sigmoid(acc)
        if scale != 1.0:
            acc = acc * scale
        if name in _ROWS16:
            stage_ref[...] = acc
            for l in range(CMP_STRIDE):
                o_ref[:, l * LANES:(l + 1) * LANES] = stage_ref[pl.ds(l, tm // CMP_STRIDE, stride=CMP_STRIDE), :]
        else:
            o_ref[...] = acc.astype(dtype)

    col = _GATE_WIDTH
    pending = []
    start = 0
    for n_groups in _PROJ_RUNS:
        specs = _PROJ_OUT[start:start + n_groups]
        width = sum(s[1] for s in specs)
        acc = _dot(x, w_ref[:, col:col + width])
        for item in pending:
            finish(*item)
        pending, off = [], 0
        for spec, o_ref in zip(specs, out_refs[start:start + n_groups]):
            pending.append((acc[:, off:off + spec[1]], spec, o_ref))
            off += spec[1]
        col += width
        start += n_groups
    for item in pending:
        finish(*item)


def _in_proj(x_bf, w, layer, cos_t, sin_prev, sin_next, seq):
    t = x_bf.shape[0]
    tm = PROJ_TM
    tiles_per_seq = seq // tm
    tab_spec = pl.BlockSpec((tm, LANES), lambda i: (i % tiles_per_seq, 0))
    fold = lambda o: CMP_STRIDE if o[0] in _ROWS16 else 1
    return pl.pallas_call(
        _proj_kernel,
        grid=(t // tm,),
        in_specs=[
            pl.BlockSpec((tm, D_MODEL), lambda i: (i, 0)),
            _layer_block(w, layer),
            tab_spec, tab_spec, tab_spec,
        ],
        out_specs=[pl.BlockSpec((tm // fold(o), o[1] * fold(o)), lambda i: (i, 0)) for o in _PROJ_OUT],
        out_shape=[jax.ShapeDtypeStruct((t // fold(o), o[1] * fold(o)), o[2]) for o in _PROJ_OUT],
        scratch_shapes=[pltpu.VMEM((tm, LANES), F32)],
        compiler_params=pltpu.CompilerParams(
            dimension_semantics=("parallel",), vmem_limit_bytes=VMEM_LIMIT),
        name="in_proj",
    )(x_bf, w, cos_t, sin_prev, sin_next)


def _gelu_tanh(x):
    return 0.5 * x * (1.0 + jnp.tanh(np.sqrt(2.0 / np.pi).astype(np.float32) * (x + 0.044715 * (x * x * x))))


def _compress_kernel(r_ref, pos_ref, w1_ref, b1_ref, w2_ref, o_ref):
    r = r_ref[0]
    ra = (r + pos_ref[0:1, :]).astype(BF16)
    rb = (r + pos_ref[1:2, :]).astype(BF16)
    p1 = _dot(ra, w1_ref[0])
    p2 = _dot(rb, w1_ref[1])
    nrow = p2.shape[0]
    hid = _gelu_tanh(p1 + pltpu.roll(p2, nrow - 1, 0) + b1_ref[...])
    o_ref[0] = _dot(hid.astype(BF16), w2_ref[...]).astype(o_ref.dtype)


def _compress(r, which, layer, pos, w1, b1, w2):
    b, nrow, width = r.shape
    full = lambda a: pl.BlockSpec((None, None) + a.shape[2:], lambda i: (which, layer) + (0,) * (a.ndim - 2),
                                  pipeline_mode=pl.Buffered(1))
    return pl.pallas_call(
        _compress_kernel,
        grid=(b,),
        in_specs=[pl.BlockSpec((1, nrow, width), lambda i: (i, 0, 0)),
                  full(pos), full(w1), full(b1), full(w2)],
        out_specs=pl.BlockSpec((1, nrow, LANES), lambda i: (i, 0, 0)),
        out_shape=jax.ShapeDtypeStruct((b, nrow, LANES), BF16),
        compiler_params=pltpu.CompilerParams(
            dimension_semantics=("parallel",), vmem_limit_bytes=VMEM_LIMIT),
        name="compress",
    )(r, pos, w1, b1, w2)


def _masked_softmax(s, mask):
    s = jnp.where(mask, s, NEG)
    m = jnp.max(s, axis=-1, keepdims=True)
    e = jnp.where(mask, jnp.exp2(s - m), 0.0)
    return e / jnp.maximum(jnp.sum(e, axis=-1, keepdims=True), 1e-30)


def _tile_rows(a, n):
    return jnp.concatenate([a] * n, axis=0)


def _nsa_kernel(q_ref, kc_ref, vc_ref, ks_ref, vs_ref, kw_ref, vw_ref, ga_ref,
                ovt_ref, et_ref, x_ref, o_ref, osel_ref):
    qi = pl.program_id(1)
    t0 = qi * Q_TILE
    hq = NSA_GROUP * Q_TILE
    nrows = NSA_HEADS * Q_TILE
    lane = lax.broadcasted_iota(jnp.int32, (1, LANES), 1)
    t_col = t0 + lax.broadcasted_iota(jnp.int32, (Q_TILE, 1), 0)
    t_rows = _tile_rows(t_col, NSA_HEADS)
    q_all = q_ref[0]

    gates = _dot(jnp.concatenate(_split_bf16(ga_ref[0]), axis=1), x_ref[...])

    j_col = lax.broadcasted_iota(jnp.int32, (N_SEL, 1), 0)
    t_lane = t0 + lax.broadcasted_iota(jnp.int32, (1, Q_TILE), 1)
    cur = t_lane // SEL_BLOCK
    forced = (j_col == 0) | (j_col == cur) | (j_col == cur - 1)
    valid = j_col * SEL_BLOCK <= t_lane

    w0 = pl.multiple_of(jnp.maximum(t0 - WINDOW, 0), Q_TILE)
    wlen = WINDOW + Q_TILE
    diff = t_col - (w0 + lax.broadcasted_iota(jnp.int32, (1, wlen), 1))
    win_bias = _tile_rows(jnp.where((diff >= 0) & (diff < WINDOW), 0.0, NEG), NSA_HEADS)
    kw = kw_ref[0, pl.ds(w0, wlen), :]
    vw = vw_ref[0, pl.ds(w0, wlen), :]

    last = qi // (SEL_KEY_TILE // Q_TILE)

    qs = jnp.concatenate(
        [jnp.where((lane // HEAD_DIM) == g, q_all[:, h * LANES:(h + 1) * LANES], jnp.zeros((), BF16))
         for g in range(NSA_KV_HEADS) for h in range(NSA_GROUP)], axis=0)

    s = _dot_nt(qs, kc_ref[0])
    m_cmp = (CMP_STRIDE * lane + CMP_BLOCK - 1) <= t_rows
    p = _masked_softmax(s, m_cmp)
    o_cmp = _dot(p.astype(BF16), vc_ref[0])

    sw = _dot_nt(qs, kw) + win_bias
    ew = jnp.exp2(sw - jnp.max(sw, axis=-1, keepdims=True)).astype(BF16)
    ow = _dot(ew, jnp.concatenate([vw, jnp.ones((wlen, LANES), BF16)], axis=1))
    o_win = ow[:, :LANES] / jnp.maximum(ow[:, LANES:], 1e-30)

    drops = []
    for g in range(NSA_KV_HEADS):
        pg = p[g * hq:(g + 1) * hq]
        psum = pg[0:Q_TILE] + pg[Q_TILE:2 * Q_TILE] + pg[2 * Q_TILE:3 * Q_TILE] + pg[3 * Q_TILE:4 * Q_TILE]
        ps_hi, ps_lo = _split_bf16(psum)
        score = _dot_nt(ovt_ref[...], ps_hi) + _dot_nt(ovt_ref[...], ps_lo)
        score = jnp.where(forced, FORCE, jnp.where(valid, score, -FORCE))
        rank = jnp.zeros((N_SEL, Q_TILE), F32)
        for jp in range(N_SEL):
            row = score[jp:jp + 1, :]
            ge = jnp.where(row >= score, 1.0, 0.0)
            gt = jnp.where(row > score, 1.0, 0.0)
            rank = rank + jnp.where(j_col > jp, ge, gt)
        drop_t = jnp.where(rank < SEL_TOPK, 0.0, NEG)
        drop_t = jnp.concatenate([drop_t, jnp.zeros((LANES - N_SEL, Q_TILE), F32)], axis=0)
        drop = drop_t.T.astype(BF16)
        drops.append(_tile_rows(drop, NSA_GROUP))
    q_aug = jnp.concatenate([qs, jnp.concatenate(drops, axis=0)], axis=1)

    crow = nrows // SEL_CHAINS
    q_parts = [q_aug[c * crow:(c + 1) * crow] for c in range(SEL_CHAINS)]

    def sel_tile(k0, carry, mask):
        k_aug = jnp.concatenate([ks_ref[0, pl.ds(k0, SEL_KEY_TILE), :],
                                 et_ref[pl.ds(k0, SEL_KEY_TILE), :]], axis=1)
        v_aug = jnp.concatenate([vs_ref[0, pl.ds(k0, SEL_KEY_TILE), :],
                                 jnp.ones((SEL_KEY_TILE, LANES), BF16)], axis=1)
        sc = [_dot_nt(q, k_aug) for q in q_parts]
        if mask is not None:
            sc = [jnp.where(mask, s_c, NEG) for s_c in sc]
        m_new = [jnp.maximum(m_i, jnp.max(s_c, axis=-1, keepdims=True)) for s_c, (m_i, _) in zip(sc, carry)]
        e = [jnp.exp2(s_c - m_c).astype(BF16) for s_c, m_c in zip(sc, m_new)]
        return tuple((m_c, jnp.exp2(m_i - m_c) * acc + _dot(e_c, v_aug))
                     for m_c, e_c, (m_i, acc) in zip(m_new, e, carry))

    for n_tiles in range(1, et_ref.shape[0] // SEL_KEY_TILE + 1):
        @pl.when(last == n_tiles - 1)
        def _():
            carry = ((jnp.full((crow, 1), NEG, F32), jnp.zeros((crow, 2 * LANES), F32)),) * SEL_CHAINS
            for kt in range(n_tiles - 1):
                carry = sel_tile(kt * SEL_KEY_TILE, carry, None)
            k_diag = (n_tiles - 1) * SEL_KEY_TILE
            causal = _tile_rows((k_diag + lax.broadcasted_iota(jnp.int32, (1, SEL_KEY_TILE), 1)) <= t_col,
                                crow // Q_TILE)
            acc = jnp.concatenate([a for _, a in sel_tile(k_diag, carry, causal)], axis=0)
            osel_ref[...] = acc[:, :LANES] / jnp.maximum(acc[:, LANES:], 1e-30)
    o_sel = osel_ref[...]

    branches = (o_cmp, o_sel, o_win)
    for h in range(NSA_GROUP):
        mixed = []
        for g in range(NSA_KV_HEADS):
            r0 = (g * NSA_GROUP + h) * Q_TILE
            acc = None
            for r in range(3):
                term = gates[:, r * NSA_Q + h * LANES: r * NSA_Q + (h + 1) * LANES] * branches[r][r0:r0 + Q_TILE]
                acc = term if acc is None else acc + term
            mixed.append(acc)
        o_ref[0, :, h * LANES:(h + 1) * LANES] = jnp.where(lane < HEAD_DIM, mixed[0], mixed[1]).astype(o_ref.dtype)


def _nsa_attention(qa, kc, vc, ksel, vsel, kwin, vwin, ga, ovt, expand, gate_x):
    b, s, _ = qa.shape
    per_q = lambda w: pl.BlockSpec((1, Q_TILE, w), lambda i, j: (i, j, 0))
    per_b = lambda a: pl.BlockSpec((1,) + a.shape[1:], lambda i, j: (i, 0, 0))
    full = _resident
    return pl.pallas_call(
        _nsa_kernel,
        grid=(b, s // Q_TILE),
        in_specs=[per_q(NSA_Q), per_b(kc), per_b(vc), per_b(ksel), per_b(vsel), per_b(kwin), per_b(vwin),
                  per_q(LANES), full(ovt), full(expand), full(gate_x)],
        out_specs=per_q(NSA_Q),
        out_shape=jax.ShapeDtypeStruct((b, s, NSA_Q), BF16),
        scratch_shapes=[pltpu.VMEM((NSA_HEADS * Q_TILE, LANES), F32)],
        compiler_params=pltpu.CompilerParams(
            dimension_semantics=("parallel", "parallel"), vmem_limit_bytes=VMEM_LIMIT),
        name="nsa_attention",
    )(qa, kc, vc, ksel, vsel, kwin, vwin, ga, ovt, expand, gate_x)


SB_SKIP = 150.0
SB_PAIRS = 4


def _sb_kernel(q_ref, k_ref, v_ref, u_ref, o_ref, run_ref, acc_ref):
    qi = pl.program_id(2)
    t0 = qi * SB_TILE
    lane = lax.broadcasted_iota(jnp.int32, (1, LANES), 1)
    t_col = t0 + lax.broadcasted_iota(jnp.int32, (SB_TILE, 1), 0)
    diag_mask = (t0 + lax.broadcasted_iota(jnp.int32, (1, SB_TILE), 1)) < t_col
    diag_mask = _tile_rows(diag_mask, 2)
    qms = []
    for p in range(SB_PAIRS):
        q = q_ref[0, :, p * LANES:(p + 1) * LANES]
        qms.append(jnp.concatenate(
            [jnp.where((lane // HEAD_DIM) == hl, q, jnp.zeros((), BF16)) for hl in range(2)], axis=0))

    def tiles(k0, state, mask):
        pairs = range(SB_PAIRS)
        cols = [slice(p * LANES, (p + 1) * LANES) for p in pairs]
        z2 = [_dot_nt(qms[p], k_ref[0, pl.ds(k0, SB_TILE), cols[p]]) for p in pairs]
        split = []
        for p in pairs:
            neg_abs = pltpu.bitcast(pltpu.bitcast(z2[p], jnp.uint32) | jnp.uint32(0x80000000), F32)
            nfail = jnp.maximum(z2[p], 0.0) + jnp.log2(1.0 + jnp.exp2(neg_abs))
            if mask is not None:
                nfail = jnp.where(mask, nfail, 0.0)
            split.append(jnp.concatenate(_split_bf16(nfail), axis=1))
        tail = [_dot(split[p], u_ref[...]) for p in pairs]
        a = []
        for p in pairs:
            ap = jnp.exp2(z2[p] + tail[p] + state[p][0])
            if mask is not None:
                ap = jnp.where(mask, ap, 0.0)
            a.append(ap.astype(BF16))
        return tuple((state[p][0] + tail[p][:, 0:1],
                      state[p][1] + _dot(a[p], v_ref[0, pl.ds(k0, SB_TILE), cols[p]])) for p in pairs)

    zero = (jnp.zeros((2 * SB_TILE, 1), F32), jnp.zeros((2 * SB_TILE, LANES), F32))

    def save(state):
        for p, (run, acc) in enumerate(state):
            run_ref[p] = run
            acc_ref[p] = acc

    @pl.when(qi == 0)
    def _():
        save(tiles(0, (zero,) * SB_PAIRS, diag_mask))

    @pl.when(qi > 0)
    def _():
        state = tiles(pl.multiple_of(t0, SB_TILE), (zero,) * SB_PAIRS, diag_mask)
        save(tiles(pl.multiple_of(t0 - SB_TILE, SB_TILE), state, None))

    state = tuple((run_ref[p], acc_ref[p]) for p in range(SB_PAIRS))

    def cond(c):
        i, state = c
        top = functools.reduce(jnp.maximum, [run for run, _ in state])
        return jnp.logical_and(i <= qi, jnp.max(top) > -SB_SKIP)

    def body(c):
        i, state = c
        k0 = pl.multiple_of((qi - i) * SB_TILE, SB_TILE)
        return i + 1, tiles(k0, state, None)

    _, state = lax.while_loop(cond, body, (jnp.int32(2), state))
    for p, (_, acc) in enumerate(state):
        o_ref[0, :, p * LANES:(p + 1) * LANES] = jnp.where(
            lane < HEAD_DIM, acc[:SB_TILE], acc[SB_TILE:]).astype(o_ref.dtype)


def _sb_attention(qb, kb, vb, upper):
    b, s, w = qb.shape
    wb = SB_PAIRS * LANES
    return pl.pallas_call(
        _sb_kernel,
        grid=(b, w // wb, s // SB_TILE),
        in_specs=[pl.BlockSpec((1, SB_TILE, wb), lambda i, j, t: (i, t, j)),
                  pl.BlockSpec((1, s, wb), lambda i, j, t: (i, 0, j)),
                  pl.BlockSpec((1, s, wb), lambda i, j, t: (i, 0, j)),
                  _resident(upper)],
        out_specs=pl.BlockSpec((1, SB_TILE, wb), lambda i, j, t: (i, t, j)),
        out_shape=jax.ShapeDtypeStruct((b, s, w), BF16),
        scratch_shapes=[pltpu.VMEM((SB_PAIRS, 2 * SB_TILE, 1), F32), pltpu.VMEM((SB_PAIRS, 2 * SB_TILE, LANES), F32)],
        compiler_params=pltpu.CompilerParams(
            dimension_semantics=("parallel", "parallel", "parallel"), vmem_limit_bytes=VMEM_LIMIT),
        name="sb_attention",
    )(qb, kb, vb, upper)


MERGE_TM = 512
MERGE_CHAINS = 2


def _merge_kernel(oa_ref, ob_ref, xb_ref, x_ref, wg_ref, wa_ref, wb_ref, wo_ref, g_ref, b_ref, y_ref, ybf_ref):
    half = xb_ref.shape[0] // MERGE_CHAINS
    rows = [slice(c * half, (c + 1) * half) for c in range(MERGE_CHAINS)]
    gate = [_dot(xb_ref[r, :], wg_ref[...]) for r in rows]
    ya = [_dot(oa_ref[r, :], wa_ref[...]) for r in rows]
    yb = [_dot(ob_ref[r, :], wb_ref[...]) for r in rows]
    merged = [jax.nn.sigmoid(g_c[:, :D_MODEL]) * a_c + jax.nn.sigmoid(g_c[:, D_MODEL:]) * b_c
              for g_c, a_c, b_c in zip(gate, ya, yb)]
    y = [_dot(m_c.astype(BF16), wo_ref[...]) for m_c in merged]
    for r, y_c in zip(rows, y):
        out = _layer_norm(DEEPNORM_ALPHA * x_ref[r, :] + y_c, g_ref[...], b_ref[...])
        y_ref[r, :] = out
        ybf_ref[r, :] = out.astype(BF16)


def _merge(oa, ob, x_bf, x, layer, wg, wa, wb, wo, g, b):
    t = x.shape[0]
    tm = MERGE_TM
    row = lambda w: pl.BlockSpec((tm, w), lambda i: (i, 0))
    full = lambda a: _layer_block(a, layer)
    gate_cols = pl.BlockSpec((None, D_MODEL, _GATE_WIDTH), lambda i: (layer, 0, 0), pipeline_mode=pl.Buffered(1))
    return pl.pallas_call(
        _merge_kernel,
        grid=(t // tm,),
        in_specs=[row(NSA_Q), row(SB_W), row(D_MODEL), row(D_MODEL),
                  gate_cols, full(wa), full(wb), full(wo), full(g), full(b)],
        out_specs=[row(D_MODEL), row(D_MODEL)],
        out_shape=[jax.ShapeDtypeStruct((t, D_MODEL), F32), jax.ShapeDtypeStruct((t, D_MODEL), BF16)],
        compiler_params=pltpu.CompilerParams(
            dimension_semantics=("parallel",), vmem_limit_bytes=VMEM_LIMIT),
        name="merge_out_ln",
    )(oa, ob, x_bf, x, wg, wa, wb, wo, g, b)


FFN_TM = 512
FFN_CHUNK = 256
HALO = 16


def _ffn_kernel(h_ref, halo_ref, x_ref, wu_ref, cw_ref, cb_ref, wd_ref, g_ref, b_ref, y_ref, ybf_ref,
                hx_ref, act_ref, *, tiles_per_seq):
    i = pl.program_id(0)
    has_prev = (i % tiles_per_seq) != 0
    tm = h_ref.shape[0]
    n_chunks = D_FF // FFN_CHUNK
    hx_ref[0:HALO, :] = jnp.where(has_prev, halo_ref[...], jnp.zeros((), BF16))
    hx_ref[HALO:, :] = h_ref[...]

    def cols_of(half, c):
        return slice(half * D_FF + c * FFN_CHUNK, half * D_FF + (c + 1) * FFN_CHUNK)

    def up(c):
        return tuple(_dot(hx_ref[...], wu_ref[:, cols_of(half, c)]) for half in range(2))

    def gate(c, us):
        def conv(half):
            cols, u = cols_of(half, c), us[half]
            return (cw_ref[0:1, cols] * pltpu.roll(u, 2, 0)[HALO:] + cw_ref[1:2, cols] * pltpu.roll(u, 1, 0)[HALO:]
                    + cw_ref[2:3, cols] * u[HALO:] + cb_ref[:, cols])
        a = conv(0)
        act_ref[:, c * FFN_CHUNK:(c + 1) * FFN_CHUNK] = (a * jax.nn.sigmoid(a) * conv(1)).astype(BF16)

    us = up(0)
    for c in range(1, n_chunks):
        nxt = up(c)
        gate(c - 1, us)
        us = nxt
    gate(n_chunks - 1, us)
    y = _dot(act_ref[...], wd_ref[...])
    out = _layer_norm(DEEPNORM_ALPHA * x_ref[...] + y, g_ref[...], b_ref[...])
    y_ref[...] = out
    ybf_ref[...] = out.astype(BF16)


def _conv_ffn(h_bf, x, layer, w_up, conv_w, conv_b, w_down, g, b, seq):
    t = x.shape[0]
    tm = FFN_TM
    row = lambda w: pl.BlockSpec((tm, w), lambda i: (i, 0))
    return pl.pallas_call(
        functools.partial(_ffn_kernel, tiles_per_seq=seq // tm),
        grid=(t // tm,),
        in_specs=[row(D_MODEL),
                  pl.BlockSpec((HALO, D_MODEL), lambda i: (jnp.maximum(i * (tm // HALO) - 1, 0), 0)),
                  row(D_MODEL),
                  *[_layer_block(a, layer) for a in (w_up, conv_w, conv_b, w_down, g, b)]],
        out_specs=[row(D_MODEL), row(D_MODEL)],
        out_shape=[jax.ShapeDtypeStruct((t, D_MODEL), F32), jax.ShapeDtypeStruct((t, D_MODEL), BF16)],
        scratch_shapes=[pltpu.VMEM((tm + HALO, D_MODEL), BF16), pltpu.VMEM((tm, D_FF), BF16)],
        compiler_params=pltpu.CompilerParams(
            dimension_semantics=("parallel",), vmem_limit_bytes=VMEM_LIMIT),
        name="conv_ffn_ln",
    )(h_bf, h_bf, x, w_up, conv_w, conv_b, w_down, g, b)


def _rope_tables(seq):
    inv_freq = ROPE_THETA ** (-np.arange(0, ROT_DIM, 2, dtype=np.float32) / ROT_DIM)
    ang = jnp.arange(seq, dtype=F32)[:, None] * jnp.asarray(inv_freq, F32)[None, :]
    cos, sin = jnp.cos(ang), jnp.sin(ang)
    half = ROT_DIM // 2
    ones = jnp.ones((seq, HEAD_DIM - ROT_DIM), F32)
    zeros = jnp.zeros((seq, HEAD_DIM - ROT_DIM), F32)
    zh = jnp.zeros((seq, half), F32)
    cos_t = jnp.concatenate([cos, cos, ones], axis=1)
    sin_prev = jnp.concatenate([zh, sin, zeros], axis=1)
    sin_next = jnp.concatenate([-sin, zh, zeros], axis=1)
    rep = LANES // HEAD_DIM
    return tuple(jnp.tile(a, (1, rep)) for a in (cos_t, sin_prev, sin_next))


_QA_PERM = np.concatenate([np.concatenate([np.arange(h * HEAD_DIM, (h + 1) * HEAD_DIM),
                                           np.arange((NSA_GROUP + h) * HEAD_DIM, (NSA_GROUP + h + 1) * HEAD_DIM)])
                           for h in range(NSA_GROUP)])


def _selection_overlap_t(seq):
    n_cmp = (seq - CMP_BLOCK) // CMP_STRIDE + 1
    n_sel = seq // SEL_BLOCK
    cs = np.arange(n_cmp) * CMP_STRIDE
    ce = cs + CMP_BLOCK
    ss = np.arange(n_sel) * SEL_BLOCK
    se = ss + SEL_BLOCK
    ov = np.clip(np.minimum(ce[:, None], se[None, :]) - np.maximum(cs[:, None], ss[None, :]), 0, None) / CMP_BLOCK
    out = np.zeros((n_sel, seq // CMP_STRIDE), np.float32)
    out[:, :n_cmp] = ov.T
    return jnp.asarray(out, BF16)


def _block_indicator(seq):
    e = np.zeros((seq, LANES), np.float32)
    e[np.arange(seq), np.arange(seq) // SEL_BLOCK] = 1.0
    return jnp.asarray(e, BF16)


def _gate_expand():
    x = np.zeros((LANES, 3 * NSA_Q), np.float32)
    for col in range(NSA_Q):
        hh = _QA_PERM[col] // HEAD_DIM
        for r in range(3):
            x[hh * 3 + r, r * NSA_Q + col] = 1.0
    return jnp.asarray(np.concatenate([x, x], axis=0), BF16)


def _neg_later_keys(n):
    u = -(np.arange(n)[:, None] >= np.arange(n)[None, :]).astype(np.float32)
    return jnp.asarray(np.concatenate([u, u], axis=0), BF16)


def _heads_to_qa_order(a, axis):
    shape = a.shape
    a = a.reshape(shape[:axis] + (NSA_KV_HEADS, NSA_GROUP, HEAD_DIM) + shape[axis + 1:])
    return jnp.swapaxes(a, axis, axis + 1).reshape(shape)


def _prep_in_proj(w):
    q_a, kc, vc, ks, vs, kw, vw, g_a, q_b, k_b, v_b, g_m = jnp.split(w.astype(BF16), SPLIT_POINTS, axis=-1)
    g_a = jnp.pad(g_a, ((0, 0), (0, 0), (0, LANES - g_a.shape[-1])))
    cols = [g_m, _heads_to_qa_order(q_a, 2), kc, ks, kw, vc, vs, vw, q_b, k_b, v_b, g_a]
    return jnp.concatenate(cols, axis=-1)


def _prep_compress(pos, w1, b1, w2):
    half = CMP_BLOCK // 2
    lead = pos.shape[:2]
    pos_t = jnp.broadcast_to(pos.reshape(lead + (2, half, 1, HEAD_DIM)),
                             lead + (2, half, NSA_KV_HEADS, HEAD_DIM)).reshape(lead + (2, half * NSA_KV))
    eye = jnp.eye(NSA_KV_HEADS, dtype=w1.dtype)
    w1r = w1.reshape(lead + (2, half, 1, HEAD_DIM, 1, CMP_HIDDEN))
    w1x = (w1r * eye[:, None, :, None]).reshape(lead + (2, half * NSA_KV, NSA_KV_HEADS * CMP_HIDDEN))
    b1t = jnp.tile(b1.reshape(lead + (1, CMP_HIDDEN)), (1, 1, 1, NSA_KV_HEADS))
    w2x = (w2.reshape(lead + (1, CMP_HIDDEN, 1, HEAD_DIM)) * eye[:, None, :, None]).reshape(
        lead + (NSA_KV_HEADS * CMP_HIDDEN, NSA_KV))
    return pos_t, w1x.astype(BF16), b1t, w2x.astype(BF16)


def kernel(x, w_in, cmp_pos_k, cmp_w1_k, cmp_b1_k, cmp_w2_k, cmp_pos_v, cmp_w1_v, cmp_b1_v, cmp_w2_v,
           w_branch_a, w_branch_b, w_out, ln_mix_g, ln_mix_b, w_up, conv_w, conv_b, w_down,
           ln_ffn_g, ln_ffn_b):
    b, s, d = x.shape
    t = b * s
    assert d == D_MODEL and s // SEL_BLOCK == N_SEL and s // CMP_STRIDE == LANES
    cos_t, sin_prev, sin_next = _rope_tables(s)
    ovt = _selection_overlap_t(s)
    expand = _block_indicator(s)
    gate_x = _gate_expand()
    upper = _neg_later_keys(SB_TILE)

    w_proj = _prep_in_proj(w_in)
    wa = _heads_to_qa_order(w_branch_a.astype(BF16), 1)
    wb = w_branch_b.astype(BF16)
    wo = w_out.astype(BF16)
    wu = w_up.astype(BF16)
    wd = w_down.astype(BF16)
    cmp_params = _prep_compress(*(jnp.stack(kv) for kv in ((cmp_pos_k, cmp_pos_v), (cmp_w1_k, cmp_w1_v),
                                                           (cmp_b1_k, cmp_b1_v), (cmp_w2_k, cmp_w2_v))))
    stack_row = lambda a: a.reshape(a.shape[0], 1, a.shape[1])
    cb, g_mix, b_mix, g_ffn, b_ffn = map(stack_row, (conv_b, ln_mix_g, ln_mix_b, ln_ffn_g, ln_ffn_b))

    xf = x.reshape(t, d)
    xb = xf.astype(BF16)
    for l in range(DEPTH):
        outs = _in_proj(xb, w_proj, l, cos_t, sin_prev, sin_next, s)
        qa, kcmp, ksel, kwin, vcmp, vsel, vwin, qb, kb, vb, ga = outs
        rows = s // CMP_STRIDE
        kc = _compress(kcmp.reshape(b, rows, CMP_STRIDE * LANES), 0, l, *cmp_params)
        vc = _compress(vcmp.reshape(b, rows, CMP_STRIDE * LANES), 1, l, *cmp_params)
        r3 = lambda a: a.reshape(b, s, a.shape[-1])
        oa = _nsa_attention(r3(qa), kc, vc, r3(ksel), r3(vsel), r3(kwin), r3(vwin), r3(ga),
                            ovt, expand, gate_x)
        ob = _sb_attention(r3(qb), r3(kb), r3(vb), upper)
        xf, xb = _merge(oa.reshape(t, NSA_Q), ob.reshape(t, SB_W), xb, xf, l,
                        w_proj, wa, wb, wo, g_mix, b_mix)
        xf, xb = _conv_ffn(xb, xf, l, wu, conv_w, cb, wd, g_ffn, b_ffn, s)
    return xf.reshape(b, s, d)
```

```python
import functools

import numpy as np
import jax
import jax.numpy as jnp
from jax import lax
from jax.experimental import pallas as pl
from jax.experimental.pallas import tpu as pltpu

D_MODEL = 1024
DEPTH = 4
HEAD_DIM = 64
NSA_HEADS = 8
NSA_KV_HEADS = 2
NSA_GROUP = NSA_HEADS // NSA_KV_HEADS
SB_HEADS = 8
CMP_BLOCK = 32
CMP_STRIDE = 16
CMP_HIDDEN = 128
SEL_BLOCK = 64
SEL_TOPK = 16
WINDOW = 512
ROPE_THETA = 500000.0
ROT_DIM = HEAD_DIM // 4
D_FF = 2816
CONV_W = 3
LN_EPS = 1e-5
NEG = -1e30
FORCE = 1e4
DEEPNORM_ALPHA = (2.0 * DEPTH) ** 0.25

NSA_Q = NSA_HEADS * HEAD_DIM
NSA_KV = NSA_KV_HEADS * HEAD_DIM
SB_W = SB_HEADS * HEAD_DIM
SPLIT_SIZES = (NSA_Q, NSA_KV, NSA_KV, NSA_KV, NSA_KV, NSA_KV, NSA_KV, 3 * NSA_HEADS, SB_W, SB_W, SB_W, 2 * D_MODEL)
SPLIT_POINTS = tuple(int(v) for v in np.cumsum(SPLIT_SIZES)[:-1])

LANES = 128
VMEM_LIMIT = 56 * 1024 * 1024

BF16 = jnp.bfloat16
F32 = jnp.float32

Q_TILE = 128
SEL_KEY_TILE = 512
SB_TILE = 256
N_SEL = 32
SEL_CHAINS = 4
WIN_CHAINS = 4


def _dot(a, b):
    return jnp.dot(a, b, preferred_element_type=F32)


def _dot_nt(a, b):
    return lax.dot_general(a, b, (((1,), (1,)), ((), ())), preferred_element_type=F32)


def _split_bf16(x):
    hi = x.astype(BF16)
    lo = (x - hi.astype(F32)).astype(BF16)
    return hi, lo


def _layer_norm(z, g, b):
    mu = jnp.mean(z, axis=-1, keepdims=True)
    zc = z - mu
    var = jnp.mean(zc * zc, axis=-1, keepdims=True)
    return zc * lax.rsqrt(var + LN_EPS) * g + b


def _resident(a):
    return pl.BlockSpec(a.shape, lambda *_: (0,) * a.ndim, pipeline_mode=pl.Buffered(1))


def _layer_block(a, l):
    return pl.BlockSpec((None,) + a.shape[1:], lambda *_: (l,) + (0,) * (a.ndim - 1), pipeline_mode=pl.Buffered(1))


LOG2E = 1.4426950408889634
Q_SCALE = HEAD_DIM ** -0.5 * LOG2E
_PROJ_OUT = (
    ("qa", NSA_Q, BF16, True, False, Q_SCALE),
    ("kcmp", LANES, F32, True, False, 1.0),
    ("ksel", LANES, BF16, True, False, 1.0),
    ("kwin", LANES, BF16, True, False, 1.0),
    ("vcmp", LANES, F32, False, False, 1.0),
    ("vsel", LANES, BF16, False, False, 1.0),
    ("vwin", LANES, BF16, False, False, 1.0),
    ("qb", SB_W, BF16, False, False, Q_SCALE),
    ("kb", SB_W, BF16, False, False, 1.0),
    ("vb", SB_W, BF16, False, False, 1.0),
    ("ga", LANES, F32, False, True, 1.0),
)
_PROJ_WIDTH = sum(o[1] for o in _PROJ_OUT)
_PROJ_RUNS = (7, 4)
_ROWS16 = ("kcmp", "vcmp")
_GATE_WIDTH = 2 * D_MODEL
PROJ_TM = 512


def _rope(acc, cos_t, sin_prev, sin_next):
    chunks = []
    for c in range(acc.shape[1] // LANES):
        xc = acc[:, c * LANES:(c + 1) * LANES]
        chunks.append(xc * cos_t + pltpu.roll(xc, ROT_DIM // 2, 1) * sin_prev
                      + pltpu.roll(xc, LANES - ROT_DIM // 2, 1) * sin_next)
    return chunks[0] if len(chunks) == 1 else jnp.concatenate(chunks, axis=1)


def _proj_kernel(x_ref, w_ref, cos_ref, sp_ref, sn_ref, *refs):
    out_refs, stage_ref = refs[:-1], refs[-1]
    x = x_ref[...]
    tm = x.shape[0]

    def finish(acc, spec, o_ref):
        name, _, dtype, rotary, sigmoid, scale = spec
        if rotary:
            acc = _rope(acc, cos_ref[...], sp_ref[...], sn_ref[...])
        if sigmoid:
            acc = jax.nn.sigmoid(acc)
        if scale != 1.0:
            acc = acc * scale
        if name in _ROWS16:
            stage_ref[...] = acc
            for l in range(CMP_STRIDE):
                o_ref[:, l * LANES:(l + 1) * LANES] = stage_ref[pl.ds(l, tm // CMP_STRIDE, stride=CMP_STRIDE), :]
        else:
            o_ref[...] = acc.astype(dtype)

    col = _GATE_WIDTH
    pending = []
    start = 0
    for n_groups in _PROJ_RUNS:
        specs = _PROJ_OUT[start:start + n_groups]
        width = sum(s[1] for s in specs)
        acc = _dot(x, w_ref[:, col:col + width])
        for item in pending:
            finish(*item)
        pending, off = [], 0
        for spec, o_ref in zip(specs, out_refs[start:start + n_groups]):
            pending.append((acc[:, off:off + spec[1]], spec, o_ref))
            off += spec[1]
        col += width
        start += n_groups
    for item in pending:
        finish(*item)


def _in_proj(x_bf, w, layer, cos_t, sin_prev, sin_next, seq):
    t = x_bf.shape[0]
    tm = PROJ_TM
    tiles_per_seq = seq // tm
    tab_spec = pl.BlockSpec((tm, LANES), lambda i: (i % tiles_per_seq, 0))
    fold = lambda o: CMP_STRIDE if o[0] in _ROWS16 else 1
    return pl.pallas_call(
        _proj_kernel,
        grid=(t // tm,),
        in_specs=[
            pl.BlockSpec((tm, D_MODEL), lambda i: (i, 0)),
            _layer_block(w, layer),
            tab_spec, tab_spec, tab_spec,
        ],
        out_specs=[pl.BlockSpec((tm // fold(o), o[1] * fold(o)), lambda i: (i, 0)) for o in _PROJ_OUT],
        out_shape=[jax.ShapeDtypeStruct((t // fold(o), o[1] * fold(o)), o[2]) for o in _PROJ_OUT],
        scratch_shapes=[pltpu.VMEM((tm, LANES), F32)],
        compiler_params=pltpu.CompilerParams(
            dimension_semantics=("parallel",), vmem_limit_bytes=VMEM_LIMIT),
        name="in_proj",
    )(x_bf, w, cos_t, sin_prev, sin_next)


def _gelu_tanh(x):
    return 0.5 * x * (1.0 + jnp.tanh(np.sqrt(2.0 / np.pi).astype(np.float32) * (x + 0.044715 * (x * x * x))))


def _compress_kernel(r_ref, pos_ref, w1_ref, b1_ref, w2_ref, o_ref):
    r = r_ref[0]
    ra = (r + pos_ref[0:1, :]).astype(BF16)
    rb = (r + pos_ref[1:2, :]).astype(BF16)
    p1 = _dot(ra, w1_ref[0])
    p2 = _dot(rb, w1_ref[1])
    nrow = p2.shape[0]
    hid = _gelu_tanh(p1 + pltpu.roll(p2, nrow - 1, 0) + b1_ref[...])
    o_ref[0] = _dot(hid.astype(BF16), w2_ref[...]).astype(o_ref.dtype)


def _compress(r, which, layer, pos, w1, b1, w2):
    b, nrow, width = r.shape
    full = lambda a: pl.BlockSpec((None, None) + a.shape[2:], lambda i: (which, layer) + (0,) * (a.ndim - 2),
                                  pipeline_mode=pl.Buffered(1))
    return pl.pallas_call(
        _compress_kernel,
        grid=(b,),
        in_specs=[pl.BlockSpec((1, nrow, width), lambda i: (i, 0, 0)),
                  full(pos), full(w1), full(b1), full(w2)],
        out_specs=pl.BlockSpec((1, nrow, LANES), lambda i: (i, 0, 0)),
        out_shape=jax.ShapeDtypeStruct((b, nrow, LANES), BF16),
        compiler_params=pltpu.CompilerParams(
            dimension_semantics=("parallel",), vmem_limit_bytes=VMEM_LIMIT),
        name="compress",
    )(r, pos, w1, b1, w2)


def _masked_softmax(s, mask):
    s = jnp.where(mask, s, NEG)
    m = jnp.max(s, axis=-1, keepdims=True)
    e = jnp.where(mask, jnp.exp2(s - m), 0.0)
    return e / jnp.maximum(jnp.sum(e, axis=-1, keepdims=True), 1e-30)


def _tile_rows(a, n):
    return jnp.concatenate([a] * n, axis=0)


def _nsa_kernel(q_ref, kc_ref, vc_ref, ks_ref, vs_ref, kw_ref, vw_ref, ga_ref,
                ovt_ref, et_ref, x_ref, o_ref, osel_ref):
    qi = pl.program_id(1)
    t0 = qi * Q_TILE
    hq = NSA_GROUP * Q_TILE
    nrows = NSA_HEADS * Q_TILE
    lane = lax.broadcasted_iota(jnp.int32, (1, LANES), 1)
    t_col = t0 + lax.broadcasted_iota(jnp.int32, (Q_TILE, 1), 0)
    t_rows = _tile_rows(t_col, NSA_HEADS)
    q_all = q_ref[0]

    gates = _dot(jnp.concatenate(_split_bf16(ga_ref[0]), axis=1), x_ref[...])

    j_col = lax.broadcasted_iota(jnp.int32, (N_SEL, 1), 0)
    t_lane = t0 + lax.broadcasted_iota(jnp.int32, (1, Q_TILE), 1)
    cur = t_lane // SEL_BLOCK
    forced = (j_col == 0) | (j_col == cur) | (j_col == cur - 1)
    valid = j_col * SEL_BLOCK <= t_lane

    w0 = pl.multiple_of(jnp.maximum(t0 - WINDOW, 0), Q_TILE)
    wlen = WINDOW + Q_TILE
    diff = t_col - (w0 + lax.broadcasted_iota(jnp.int32, (1, wlen), 1))
    win_bias = _tile_rows(jnp.where((diff >= 0) & (diff < WINDOW), 0.0, NEG), NSA_HEADS)
    kw = kw_ref[0, pl.ds(w0, wlen), :]
    vw = vw_ref[0, pl.ds(w0, wlen), :]

    last = qi // (SEL_KEY_TILE // Q_TILE)

    qs = jnp.concatenate(
        [jnp.where((lane // HEAD_DIM) == g, q_all[:, h * LANES:(h + 1) * LANES], jnp.zeros((), BF16))
         for g in range(NSA_KV_HEADS) for h in range(NSA_GROUP)], axis=0)

    s = _dot_nt(qs, kc_ref[0])
    m_cmp = (CMP_STRIDE * lane + CMP_BLOCK - 1) <= t_rows
    p = _masked_softmax(s, m_cmp)
    o_cmp = _dot(p.astype(BF16), vc_ref[0])

    wrow = nrows // WIN_CHAINS
    win_bias = win_bias[:wrow]
    vw_aug = jnp.concatenate([vw, jnp.ones((wlen, LANES), BF16)], axis=1)
    sw = [_dot_nt(qs[c * wrow:(c + 1) * wrow], kw) + win_bias for c in range(WIN_CHAINS)]
    ew = [jnp.exp2(s_c - jnp.max(s_c, axis=-1, keepdims=True)).astype(BF16) for s_c in sw]
    ow = jnp.concatenate([_dot(e_c, vw_aug) for e_c in ew], axis=0)
    o_win = ow[:, :LANES] / jnp.maximum(ow[:, LANES:], 1e-30)

    drops = []
    for g in range(NSA_KV_HEADS):
        pg = p[g * hq:(g + 1) * hq]
        psum = pg[0:Q_TILE] + pg[Q_TILE:2 * Q_TILE] + pg[2 * Q_TILE:3 * Q_TILE] + pg[3 * Q_TILE:4 * Q_TILE]
        ps_hi, ps_lo = _split_bf16(psum)
        score = _dot_nt(ovt_ref[...], ps_hi) + _dot_nt(ovt_ref[...], ps_lo)
        score = jnp.where(forced, FORCE, jnp.where(valid, score, -FORCE))
        rank = jnp.zeros((N_SEL, Q_TILE), F32)
        for jp in range(N_SEL):
            row = score[jp:jp + 1, :]
            ge = jnp.where(row >= score, 1.0, 0.0)
            gt = jnp.where(row > score, 1.0, 0.0)
            rank = rank + jnp.where(j_col > jp, ge, gt)
        drop_t = jnp.where(rank < SEL_TOPK, 0.0, NEG)
        drop_t = jnp.concatenate([drop_t, jnp.zeros((LANES - N_SEL, Q_TILE), F32)], axis=0)
        drop = drop_t.T.astype(BF16)
        drops.append(_tile_rows(drop, NSA_GROUP))
    q_aug = jnp.concatenate([qs, jnp.concatenate(drops, axis=0)], axis=1)

    crow = nrows // SEL_CHAINS
    q_parts = [q_aug[c * crow:(c + 1) * crow] for c in range(SEL_CHAINS)]

    def sel_tile(k0, carry, mask):
        k_aug = jnp.concatenate([ks_ref[0, pl.ds(k0, SEL_KEY_TILE), :],
                                 et_ref[pl.ds(k0, SEL_KEY_TILE), :]], axis=1)
        v_aug = jnp.concatenate([vs_ref[0, pl.ds(k0, SEL_KEY_TILE), :],
                                 jnp.ones((SEL_KEY_TILE, LANES), BF16)], axis=1)
        sc = [_dot_nt(q, k_aug) for q in q_parts]
        if mask is not None:
            sc = [jnp.where(mask, s_c, NEG) for s_c in sc]
        m_new = [jnp.maximum(m_i, jnp.max(s_c, axis=-1, keepdims=True)) for s_c, (m_i, _) in zip(sc, carry)]
        e = [jnp.exp2(s_c - m_c).astype(BF16) for s_c, m_c in zip(sc, m_new)]
        return tuple((m_c, jnp.exp2(m_i - m_c) * acc + _dot(e_c, v_aug))
                     for m_c, e_c, (m_i, acc) in zip(m_new, e, carry))

    for n_tiles in range(1, et_ref.shape[0] // SEL_KEY_TILE + 1):
        @pl.when(last == n_tiles - 1)
        def _():
            carry = ((jnp.full((crow, 1), NEG, F32), jnp.zeros((crow, 2 * LANES), F32)),) * SEL_CHAINS
            for kt in range(n_tiles - 1):
                carry = sel_tile(kt * SEL_KEY_TILE, carry, None)
            k_diag = (n_tiles - 1) * SEL_KEY_TILE
            causal = _tile_rows((k_diag + lax.broadcasted_iota(jnp.int32, (1, SEL_KEY_TILE), 1)) <= t_col,
                                crow // Q_TILE)
            acc = jnp.concatenate([a for _, a in sel_tile(k_diag, carry, causal)], axis=0)
            osel_ref[...] = acc[:, :LANES] / jnp.maximum(acc[:, LANES:], 1e-30)
    o_sel = osel_ref[...]

    branches = (o_cmp, o_sel, o_win)
    for h in range(NSA_GROUP):
        mixed = []
        for g in range(NSA_KV_HEADS):
            r0 = (g * NSA_GROUP + h) * Q_TILE
            acc = None
            for r in range(3):
                term = gates[:, r * NSA_Q + h * LANES: r * NSA_Q + (h + 1) * LANES] * branches[r][r0:r0 + Q_TILE]
                acc = term if acc is None else acc + term
            mixed.append(acc)
        o_ref[0, :, h * LANES:(h + 1) * LANES] = jnp.where(lane < HEAD_DIM, mixed[0], mixed[1]).astype(o_ref.dtype)


def _nsa_attention(qa, kc, vc, ksel, vsel, kwin, vwin, ga, ovt, expand, gate_x):
    b, s, _ = qa.shape
    per_q = lambda w: pl.BlockSpec((1, Q_TILE, w), lambda i, j: (i, j, 0))
    per_b = lambda a: pl.BlockSpec((1,) + a.shape[1:], lambda i, j: (i, 0, 0))
    full = _resident
    return pl.pallas_call(
        _nsa_kernel,
        grid=(b, s // Q_TILE),
        in_specs=[per_q(NSA_Q), per_b(kc), per_b(vc), per_b(ksel), per_b(vsel), per_b(kwin), per_b(vwin),
                  per_q(LANES), full(ovt), full(expand), full(gate_x)],
        out_specs=per_q(NSA_Q),
        out_shape=jax.ShapeDtypeStruct((b, s, NSA_Q), BF16),
        scratch_shapes=[pltpu.VMEM((NSA_HEADS * Q_TILE, LANES), F32)],
        compiler_params=pltpu.CompilerParams(
            dimension_semantics=("parallel", "parallel"), vmem_limit_bytes=VMEM_LIMIT),
        name="nsa_attention",
    )(qa, kc, vc, ksel, vsel, kwin, vwin, ga, ovt, expand, gate_x)


SB_SKIP = 150.0
SB_PAIRS = 4


def _sb_kernel(q_ref, k_ref, v_ref, u_ref, o_ref, run_ref, acc_ref):
    qi = pl.program_id(2)
    t0 = qi * SB_TILE
    lane = lax.broadcasted_iota(jnp.int32, (1, LANES), 1)
    t_col = t0 + lax.broadcasted_iota(jnp.int32, (SB_TILE, 1), 0)
    diag_mask = (t0 + lax.broadcasted_iota(jnp.int32, (1, SB_TILE), 1)) < t_col
    diag_mask = _tile_rows(diag_mask, 2)
    qms = []
    for p in range(SB_PAIRS):
        q = q_ref[0, :, p * LANES:(p + 1) * LANES]
        qms.append(jnp.concatenate(
            [jnp.where((lane // HEAD_DIM) == hl, q, jnp.zeros((), BF16)) for hl in range(2)], axis=0))

    def tiles(k0, state, mask):
        pairs = range(SB_PAIRS)
        cols = [slice(p * LANES, (p + 1) * LANES) for p in pairs]
        z2 = [_dot_nt(qms[p], k_ref[0, pl.ds(k0, SB_TILE), cols[p]]) for p in pairs]
        split = []
        for p in pairs:
            neg_abs = pltpu.bitcast(pltpu.bitcast(z2[p], jnp.uint32) | jnp.uint32(0x80000000), F32)
            nfail = jnp.maximum(z2[p], 0.0) + jnp.log2(1.0 + jnp.exp2(neg_abs))
            if mask is not None:
                nfail = jnp.where(mask, nfail, 0.0)
            split.append(jnp.concatenate(_split_bf16(nfail), axis=1))
        tail = [_dot(split[p], u_ref[...]) for p in pairs]
        a = []
        for p in pairs:
            ap = jnp.exp2(z2[p] + tail[p] + state[p][0])
            if mask is not None:
                ap = jnp.where(mask, ap, 0.0)
            a.append(ap.astype(BF16))
        return tuple((state[p][0] + tail[p][:, 0:1],
                      state[p][1] + _dot(a[p], v_ref[0, pl.ds(k0, SB_TILE), cols[p]])) for p in pairs)

    zero = (jnp.zeros((2 * SB_TILE, 1), F32), jnp.zeros((2 * SB_TILE, LANES), F32))

    def save(state):
        for p, (run, acc) in enumerate(state):
            run_ref[p] = run
            acc_ref[p] = acc

    @pl.when(qi == 0)
    def _():
        save(tiles(0, (zero,) * SB_PAIRS, diag_mask))

    @pl.when(qi > 0)
    def _():
        state = tiles(pl.multiple_of(t0, SB_TILE), (zero,) * SB_PAIRS, diag_mask)
        save(tiles(pl.multiple_of(t0 - SB_TILE, SB_TILE), state, None))

    state = tuple((run_ref[p], acc_ref[p]) for p in range(SB_PAIRS))

    def cond(c):
        i, state = c
        top = functools.reduce(jnp.maximum, [run for run, _ in state])
        return jnp.logical_and(i <= qi, jnp.max(top) > -SB_SKIP)

    def body(c):
        i, state = c
        k0 = pl.multiple_of((qi - i) * SB_TILE, SB_TILE)
        return i + 1, tiles(k0, state, None)

    _, state = lax.while_loop(cond, body, (jnp.int32(2), state))
    for p, (_, acc) in enumerate(state):
        o_ref[0, :, p * LANES:(p + 1) * LANES] = jnp.where(
            lane < HEAD_DIM, acc[:SB_TILE], acc[SB_TILE:]).astype(o_ref.dtype)


def _sb_attention(qb, kb, vb, upper):
    b, s, w = qb.shape
    wb = SB_PAIRS * LANES
    return pl.pallas_call(
        _sb_kernel,
        grid=(b, w // wb, s // SB_TILE),
        in_specs=[pl.BlockSpec((1, SB_TILE, wb), lambda i, j, t: (i, t, j)),
                  pl.BlockSpec((1, s, wb), lambda i, j, t: (i, 0, j)),
                  pl.BlockSpec((1, s, wb), lambda i, j, t: (i, 0, j)),
                  _resident(upper)],
        out_specs=pl.BlockSpec((1, SB_TILE, wb), lambda i, j, t: (i, t, j)),
        out_shape=jax.ShapeDtypeStruct((b, s, w), BF16),
        scratch_shapes=[pltpu.VMEM((SB_PAIRS, 2 * SB_TILE, 1), F32), pltpu.VMEM((SB_PAIRS, 2 * SB_TILE, LANES), F32)],
        compiler_params=pltpu.CompilerParams(
            dimension_semantics=("parallel", "parallel", "parallel"), vmem_limit_bytes=VMEM_LIMIT),
        name="sb_attention",
    )(qb, kb, vb, upper)


MERGE_TM = 512
MERGE_CHAINS = 2


def _merge_kernel(oa_ref, ob_ref, xb_ref, x_ref, wg_ref, wa_ref, wb_ref, wo_ref, g_ref, b_ref, y_ref, ybf_ref):
    half = xb_ref.shape[0] // MERGE_CHAINS
    rows = [slice(c * half, (c + 1) * half) for c in range(MERGE_CHAINS)]
    gate = [_dot(xb_ref[r, :], wg_ref[...]) for r in rows]
    ya = [_dot(oa_ref[r, :], wa_ref[...]) for r in rows]
    yb = [_dot(ob_ref[r, :], wb_ref[...]) for r in rows]
    merged = [jax.nn.sigmoid(g_c[:, :D_MODEL]) * a_c + jax.nn.sigmoid(g_c[:, D_MODEL:]) * b_c
              for g_c, a_c, b_c in zip(gate, ya, yb)]
    y = [_dot(m_c.astype(BF16), wo_ref[...]) for m_c in merged]
    for r, y_c in zip(rows, y):
        out = _layer_norm(DEEPNORM_ALPHA * x_ref[r, :] + y_c, g_ref[...], b_ref[...])
        y_ref[r, :] = out
        ybf_ref[r, :] = out.astype(BF16)


def _merge(oa, ob, x_bf, x, layer, wg, wa, wb, wo, g, b):
    t = x.shape[0]
    tm = MERGE_TM
    row = lambda w: pl.BlockSpec((tm, w), lambda i: (i, 0))
    full = lambda a: _layer_block(a, layer)
    gate_cols = pl.BlockSpec((None, D_MODEL, _GATE_WIDTH), lambda i: (layer, 0, 0), pipeline_mode=pl.Buffered(1))
    return pl.pallas_call(
        _merge_kernel,
        grid=(t // tm,),
        in_specs=[row(NSA_Q), row(SB_W), row(D_MODEL), row(D_MODEL),
                  gate_cols, full(wa), full(wb), full(wo), full(g), full(b)],
        out_specs=[row(D_MODEL), row(D_MODEL)],
        out_shape=[jax.ShapeDtypeStruct((t, D_MODEL), F32), jax.ShapeDtypeStruct((t, D_MODEL), BF16)],
        compiler_params=pltpu.CompilerParams(
            dimension_semantics=("parallel",), vmem_limit_bytes=VMEM_LIMIT),
        name="merge_out_ln",
    )(oa, ob, x_bf, x, wg, wa, wb, wo, g, b)


FFN_TM = 512
FFN_CHUNK = 256
HALO = 16


def _ffn_kernel(h_ref, halo_ref, x_ref, wu_ref, cw_ref, cb_ref, wd_ref, g_ref, b_ref, y_ref, ybf_ref,
                hx_ref, act_ref, *, tiles_per_seq):
    i = pl.program_id(0)
    has_prev = (i % tiles_per_seq) != 0
    tm = h_ref.shape[0]
    n_chunks = D_FF // FFN_CHUNK
    hx_ref[0:HALO, :] = jnp.where(has_prev, halo_ref[...], jnp.zeros((), BF16))
    hx_ref[HALO:, :] = h_ref[...]

    def cols_of(half, c):
        return slice(half * D_FF + c * FFN_CHUNK, half * D_FF + (c + 1) * FFN_CHUNK)

    def up(c):
        return tuple(_dot(hx_ref[...], wu_ref[:, cols_of(half, c)]) for half in range(2))

    def gate(c, us):
        def conv(half):
            cols, u = cols_of(half, c), us[half]
            return (cw_ref[0:1, cols] * pltpu.roll(u, 2, 0)[HALO:] + cw_ref[1:2, cols] * pltpu.roll(u, 1, 0)[HALO:]
                    + cw_ref[2:3, cols] * u[HALO:] + cb_ref[:, cols])
        a = conv(0)
        act_ref[:, c * FFN_CHUNK:(c + 1) * FFN_CHUNK] = (a * jax.nn.sigmoid(a) * conv(1)).astype(BF16)

    us = up(0)
    for c in range(1, n_chunks):
        nxt = up(c)
        gate(c - 1, us)
        us = nxt
    gate(n_chunks - 1, us)
    y = _dot(act_ref[...], wd_ref[...])
    out = _layer_norm(DEEPNORM_ALPHA * x_ref[...] + y, g_ref[...], b_ref[...])
    y_ref[...] = out
    ybf_ref[...] = out.astype(BF16)


def _conv_ffn(h_bf, x, layer, w_up, conv_w, conv_b, w_down, g, b, seq):
    t = x.shape[0]
    tm = FFN_TM
    row = lambda w: pl.BlockSpec((tm, w), lambda i: (i, 0))
    return pl.pallas_call(
        functools.partial(_ffn_kernel, tiles_per_seq=seq // tm),
        grid=(t // tm,),
        in_specs=[row(D_MODEL),
                  pl.BlockSpec((HALO, D_MODEL), lambda i: (jnp.maximum(i * (tm // HALO) - 1, 0), 0)),
                  row(D_MODEL),
                  *[_layer_block(a, layer) for a in (w_up, conv_w, conv_b, w_down, g, b)]],
        out_specs=[row(D_MODEL), row(D_MODEL)],
        out_shape=[jax.ShapeDtypeStruct((t, D_MODEL), F32), jax.ShapeDtypeStruct((t, D_MODEL), BF16)],
        scratch_shapes=[pltpu.VMEM((tm + HALO, D_MODEL), BF16), pltpu.VMEM((tm, D_FF), BF16)],
        compiler_params=pltpu.CompilerParams(
            dimension_semantics=("parallel",), vmem_limit_bytes=VMEM_LIMIT),
        name="conv_ffn_ln",
    )(h_bf, h_bf, x, w_up, conv_w, conv_b, w_down, g, b)


def _rope_tables(seq):
    inv_freq = ROPE_THETA ** (-np.arange(0, ROT_DIM, 2, dtype=np.float32) / ROT_DIM)
    ang = jnp.arange(seq, dtype=F32)[:, None] * jnp.asarray(inv_freq, F32)[None, :]
    cos, sin = jnp.cos(ang), jnp.sin(ang)
    half = ROT_DIM // 2
    ones = jnp.ones((seq, HEAD_DIM - ROT_DIM), F32)
    zeros = jnp.zeros((seq, HEAD_DIM - ROT_DIM), F32)
    zh = jnp.zeros((seq, half), F32)
    cos_t = jnp.concatenate([cos, cos, ones], axis=1)
    sin_prev = jnp.concatenate([zh, sin, zeros], axis=1)
    sin_next = jnp.concatenate([-sin, zh, zeros], axis=1)
    rep = LANES // HEAD_DIM
    return tuple(jnp.tile(a, (1, rep)) for a in (cos_t, sin_prev, sin_next))


_QA_PERM = np.concatenate([np.concatenate([np.arange(h * HEAD_DIM, (h + 1) * HEAD_DIM),
                                           np.arange((NSA_GROUP + h) * HEAD_DIM, (NSA_GROUP + h + 1) * HEAD_DIM)])
                           for h in range(NSA_GROUP)])


def _selection_overlap_t(seq):
    n_cmp = (seq - CMP_BLOCK) // CMP_STRIDE + 1
    n_sel = seq // SEL_BLOCK
    cs = np.arange(n_cmp) * CMP_STRIDE
    ce = cs + CMP_BLOCK
    ss = np.arange(n_sel) * SEL_BLOCK
    se = ss + SEL_BLOCK
    ov = np.clip(np.minimum(ce[:, None], se[None, :]) - np.maximum(cs[:, None], ss[None, :]), 0, None) / CMP_BLOCK
    out = np.zeros((n_sel, seq // CMP_STRIDE), np.float32)
    out[:, :n_cmp] = ov.T
    return jnp.asarray(out, BF16)


def _block_indicator(seq):
    e = np.zeros((seq, LANES), np.float32)
    e[np.arange(seq), np.arange(seq) // SEL_BLOCK] = 1.0
    return jnp.asarray(e, BF16)


def _gate_expand():
    x = np.zeros((LANES, 3 * NSA_Q), np.float32)
    for col in range(NSA_Q):
        hh = _QA_PERM[col] // HEAD_DIM
        for r in range(3):
            x[hh * 3 + r, r * NSA_Q + col] = 1.0
    return jnp.asarray(np.concatenate([x, x], axis=0), BF16)


def _neg_later_keys(n):
    u = -(np.arange(n)[:, None] >= np.arange(n)[None, :]).astype(np.float32)
    return jnp.asarray(np.concatenate([u, u], axis=0), BF16)


def _heads_to_qa_order(a, axis):
    shape = a.shape
    a = a.reshape(shape[:axis] + (NSA_KV_HEADS, NSA_GROUP, HEAD_DIM) + shape[axis + 1:])
    return jnp.swapaxes(a, axis, axis + 1).reshape(shape)


def _prep_in_proj(w):
    q_a, kc, vc, ks, vs, kw, vw, g_a, q_b, k_b, v_b, g_m = jnp.split(w.astype(BF16), SPLIT_POINTS, axis=-1)
    g_a = jnp.pad(g_a, ((0, 0), (0, 0), (0, LANES - g_a.shape[-1])))
    cols = [g_m, _heads_to_qa_order(q_a, 2), kc, ks, kw, vc, vs, vw, q_b, k_b, v_b, g_a]
    return jnp.concatenate(cols, axis=-1)


def _prep_compress(pos, w1, b1, w2):
    half = CMP_BLOCK // 2
    lead = pos.shape[:2]
    pos_t = jnp.broadcast_to(pos.reshape(lead + (2, half, 1, HEAD_DIM)),
                             lead + (2, half, NSA_KV_HEADS, HEAD_DIM)).reshape(lead + (2, half * NSA_KV))
    eye = jnp.eye(NSA_KV_HEADS, dtype=w1.dtype)
    w1r = w1.reshape(lead + (2, half, 1, HEAD_DIM, 1, CMP_HIDDEN))
    w1x = (w1r * eye[:, None, :, None]).reshape(lead + (2, half * NSA_KV, NSA_KV_HEADS * CMP_HIDDEN))
    b1t = jnp.tile(b1.reshape(lead + (1, CMP_HIDDEN)), (1, 1, 1, NSA_KV_HEADS))
    w2x = (w2.reshape(lead + (1, CMP_HIDDEN, 1, HEAD_DIM)) * eye[:, None, :, None]).reshape(
        lead + (NSA_KV_HEADS * CMP_HIDDEN, NSA_KV))
    return pos_t, w1x.astype(BF16), b1t, w2x.astype(BF16)


def kernel(x, w_in, cmp_pos_k, cmp_w1_k, cmp_b1_k, cmp_w2_k, cmp_pos_v, cmp_w1_v, cmp_b1_v, cmp_w2_v,
           w_branch_a, w_branch_b, w_out, ln_mix_g, ln_mix_b, w_up, conv_w, conv_b, w_down,
           ln_ffn_g, ln_ffn_b):
    b, s, d = x.shape
    t = b * s
    assert d == D_MODEL and s // SEL_BLOCK == N_SEL and s // CMP_STRIDE == LANES
    cos_t, sin_prev, sin_next = _rope_tables(s)
    ovt = _selection_overlap_t(s)
    expand = _block_indicator(s)
    gate_x = _gate_expand()
    upper = _neg_later_keys(SB_TILE)

    w_proj = _prep_in_proj(w_in)
    wa = _heads_to_qa_order(w_branch_a.astype(BF16), 1)
    wb = w_branch_b.astype(BF16)
    wo = w_out.astype(BF16)
    wu = w_up.astype(BF16)
    wd = w_down.astype(BF16)
    cmp_params = _prep_compress(*(jnp.stack(kv) for kv in ((cmp_pos_k, cmp_pos_v), (cmp_w1_k, cmp_w1_v),
                                                           (cmp_b1_k, cmp_b1_v), (cmp_w2_k, cmp_w2_v))))
    stack_row = lambda a: a.reshape(a.shape[0], 1, a.shape[1])
    cb, g_mix, b_mix, g_ffn, b_ffn = map(stack_row, (conv_b, ln_mix_g, ln_mix_b, ln_ffn_g, ln_ffn_b))

    xf = x.reshape(t, d)
    xb = xf.astype(BF16)
    for l in range(DEPTH):
        outs = _in_proj(xb, w_proj, l, cos_t, sin_prev, sin_next, s)
        qa, kcmp, ksel, kwin, vcmp, vsel, vwin, qb, kb, vb, ga = outs
        rows = s // CMP_STRIDE
        kc = _compress(kcmp.reshape(b, rows, CMP_STRIDE * LANES), 0, l, *cmp_params)
        vc = _compress(vcmp.reshape(b, rows, CMP_STRIDE * LANES), 1, l, *cmp_params)
        r3 = lambda a: a.reshape(b, s, a.shape[-1])
        oa = _nsa_attention(r3(qa), kc, vc, r3(ksel), r3(vsel), r3(kwin), r3(vwin), r3(ga),
                            ovt, expand, gate_x)
        ob = _sb_attention(r3(qb), r3(kb), r3(vb), upper)
        xf, xb = _merge(oa.reshape(t, NSA_Q), ob.reshape(t, SB_W), xb, xf, l,
                        w_proj, wa, wb, wo, g_mix, b_mix)
        xf, xb = _conv_ffn(xb, xf, l, wu, conv_w, cb, wd, g_ffn, b_ffn, s)
    return xf.reshape(b, s, d)
```

```python
import functools

import numpy as np
import jax
import jax.numpy as jnp
from jax import lax
from jax.experimental import pallas as pl
from jax.experimental.pallas import tpu as pltpu

D_MODEL = 1024
DEPTH = 4
HEAD_DIM = 64
NSA_HEADS = 8
NSA_KV_HEADS = 2
NSA_GROUP = NSA_HEADS // NSA_KV_HEADS
SB_HEADS = 8
CMP_BLOCK = 32
CMP_STRIDE = 16
CMP_HIDDEN = 128
SEL_BLOCK = 64
SEL_TOPK = 16
WINDOW = 512
ROPE_THETA = 500000.0
ROT_DIM = HEAD_DIM // 4
D_FF = 2816
CONV_W = 3
LN_EPS = 1e-5
NEG = -1e30
FORCE = 1e4
DEEPNORM_ALPHA = (2.0 * DEPTH) ** 0.25

NSA_Q = NSA_HEADS * HEAD_DIM
NSA_KV = NSA_KV_HEADS * HEAD_DIM
SB_W = SB_HEADS * HEAD_DIM
SPLIT_SIZES = (NSA_Q, NSA_KV, NSA_KV, NSA_KV, NSA_KV, NSA_KV, NSA_KV, 3 * NSA_HEADS, SB_W, SB_W, SB_W, 2 * D_MODEL)
SPLIT_POINTS = tuple(int(v) for v in np.cumsum(SPLIT_SIZES)[:-1])

LANES = 128
VMEM_LIMIT = 56 * 1024 * 1024

BF16 = jnp.bfloat16
F32 = jnp.float32

Q_TILE = 128
SEL_KEY_TILE = 512
SB_TILE = 256
N_SEL = 32
SEL_CHAINS = 4
WIN_CHAINS = 4


def _dot(a, b):
    return jnp.dot(a, b, preferred_element_type=F32)


def _dot_nt(a, b):
    return lax.dot_general(a, b, (((1,), (1,)), ((), ())), preferred_element_type=F32)


def _split_bf16(x):
    hi = x.astype(BF16)
    lo = (x - hi.astype(F32)).astype(BF16)
    return hi, lo


def _layer_norm(z, g, b):
    mu = jnp.mean(z, axis=-1, keepdims=True)
    zc = z - mu
    var = jnp.mean(zc * zc, axis=-1, keepdims=True)
    return zc * lax.rsqrt(var + LN_EPS) * g + b


def _resident(a):
    return pl.BlockSpec(a.shape, lambda *_: (0,) * a.ndim, pipeline_mode=pl.Buffered(1))


def _layer_block(a, l):
    return pl.BlockSpec((None,) + a.shape[1:], lambda *_: (l,) + (0,) * (a.ndim - 1), pipeline_mode=pl.Buffered(1))


LOG2E = 1.4426950408889634
Q_SCALE = HEAD_DIM ** -0.5 * LOG2E
_PROJ_OUT = (
    ("qa", NSA_Q, BF16, True, False, Q_SCALE),
    ("kcmp", LANES, F32, True, False, 1.0),
    ("ksel", LANES, BF16, True, False, 1.0),
    ("kwin", LANES, BF16, True, False, 1.0),
    ("vcmp", LANES, F32, False, False, 1.0),
    ("vsel", LANES, BF16, False, False, 1.0),
    ("vwin", LANES, BF16, False, False, 1.0),
    ("qb", SB_W, BF16, False, False, Q_SCALE),
    ("kb", SB_W, BF16, False, False, 1.0),
    ("vb", SB_W, BF16, False, False, 1.0),
    ("ga", LANES, F32, False, True, 1.0),
)
_PROJ_WIDTH = sum(o[1] for o in _PROJ_OUT)
_PROJ_RUNS = (7, 4)
_ROWS16 = ("kcmp", "vcmp")
_GATE_WIDTH = 2 * D_MODEL
PROJ_TM = 512


def _rope(acc, cos_t, sin_prev, sin_next):
    chunks = []
    for c in range(acc.shape[1] // LANES):
        xc = acc[:, c * LANES:(c + 1) * LANES]
        chunks.append(xc * cos_t + pltpu.roll(xc, ROT_DIM // 2, 1) * sin_prev
                      + pltpu.roll(xc, LANES - ROT_DIM // 2, 1) * sin_next)
    return chunks[0] if len(chunks) == 1 else jnp.concatenate(chunks, axis=1)


def _proj_kernel(x_ref, w_ref, cos_ref, sp_ref, sn_ref, *refs):
    out_refs, stage_ref = refs[:-1], refs[-1]
    x = x_ref[...]
    tm = x.shape[0]

    def finish(acc, spec, o_ref):
        name, _, dtype, rotary, sigmoid, scale = spec
        if rotary:
            acc = _rope(acc, cos_ref[...], sp_ref[...], sn_ref[...])
        if sigmoid:
            acc = jax.nn.sigmoid(acc)
        if scale != 1.0:
            acc = acc * scale
        if name in _ROWS16:
            stage_ref[...] = acc
            for l in range(CMP_STRIDE):
                o_ref[:, l * LANES:(l + 1) * LANES] = stage_ref[pl.ds(l, tm // CMP_STRIDE, stride=CMP_STRIDE), :]
        else:
            o_ref[...] = acc.astype(dtype)

    col = _GATE_WIDTH
    pending = []
    start = 0
    for n_groups in _PROJ_RUNS:
        specs = _PROJ_OUT[start:start + n_groups]
        width = sum(s[1] for s in specs)
        acc = _dot(x, w_ref[:, col:col + width])
        for item in pending:
            finish(*item)
        pending, off = [], 0
        for spec, o_ref in zip(specs, out_refs[start:start + n_groups]):
            pending.append((acc[:, off:off + spec[1]], spec, o_ref))
            off += spec[1]
        col += width
        start += n_groups
    for item in pending:
        finish(*item)


def _in_proj(x_bf, w, layer, cos_t, sin_prev, sin_next, seq):
    t = x_bf.shape[0]
    tm = PROJ_TM
    tiles_per_seq = seq // tm
    tab_spec = pl.BlockSpec((tm, LANES), lambda i: (i % tiles_per_seq, 0))
    fold = lambda o: CMP_STRIDE if o[0] in _ROWS16 else 1
    return pl.pallas_call(
        _proj_kernel,
        grid=(t // tm,),
        in_specs=[
            pl.BlockSpec((tm, D_MODEL), lambda i: (i, 0)),
            _layer_block(w, layer),
            tab_spec, tab_spec, tab_spec,
        ],
        out_specs=[pl.BlockSpec((tm // fold(o), o[1] * fold(o)), lambda i: (i, 0)) for o in _PROJ_OUT],
        out_shape=[jax.ShapeDtypeStruct((t // fold(o), o[1] * fold(o)), o[2]) for o in _PROJ_OUT],
        scratch_shapes=[pltpu.VMEM((tm, LANES), F32)],
        compiler_params=pltpu.CompilerParams(
            dimension_semantics=("parallel",), vmem_limit_bytes=VMEM_LIMIT),
        name="in_proj",
    )(x_bf, w, cos_t, sin_prev, sin_next)


def _gelu_tanh(x):
    return 0.5 * x * (1.0 + jnp.tanh(np.sqrt(2.0 / np.pi).astype(np.float32) * (x + 0.044715 * (x * x * x))))


def _compress_kernel(r_ref, pos_ref, w1_ref, b1_ref, w2_ref, o_ref):
    r = r_ref[0]
    ra = (r + pos_ref[0:1, :]).astype(BF16)
    rb = (r + pos_ref[1:2, :]).astype(BF16)
    p1 = _dot(ra, w1_ref[0])
    p2 = _dot(rb, w1_ref[1])
    nrow = p2.shape[0]
    hid = _gelu_tanh(p1 + pltpu.roll(p2, nrow - 1, 0) + b1_ref[...])
    o_ref[0] = _dot(hid.astype(BF16), w2_ref[...]).astype(o_ref.dtype)


def _compress(r, which, layer, pos, w1, b1, w2):
    b, nrow, width = r.shape
    full = lambda a: pl.BlockSpec((None, None) + a.shape[2:], lambda i: (which, layer) + (0,) * (a.ndim - 2),
                                  pipeline_mode=pl.Buffered(1))
    return pl.pallas_call(
        _compress_kernel,
        grid=(b,),
        in_specs=[pl.BlockSpec((1, nrow, width), lambda i: (i, 0, 0)),
                  full(pos), full(w1), full(b1), full(w2)],
        out_specs=pl.BlockSpec((1, nrow, LANES), lambda i: (i, 0, 0)),
        out_shape=jax.ShapeDtypeStruct((b, nrow, LANES), BF16),
        compiler_params=pltpu.CompilerParams(
            dimension_semantics=("parallel",), vmem_limit_bytes=VMEM_LIMIT),
        name="compress",
    )(r, pos, w1, b1, w2)


def _masked_softmax(s, mask):
    s = jnp.where(mask, s, NEG)
    m = jnp.max(s, axis=-1, keepdims=True)
    e = jnp.where(mask, jnp.exp2(s - m), 0.0)
    return e / jnp.maximum(jnp.sum(e, axis=-1, keepdims=True), 1e-30)


def _tile_rows(a, n):
    return jnp.concatenate([a] * n, axis=0)


def _nsa_kernel(q_ref, kc_ref, vc_ref, ks_ref, vs_ref, kw_ref, vw_ref, ga_ref,
                ovt_ref, et_ref, x_ref, o_ref, osel_ref):
    qi = pl.program_id(1)
    t0 = qi * Q_TILE
    hq = NSA_GROUP * Q_TILE
    nrows = NSA_HEADS * Q_TILE
    lane = lax.broadcasted_iota(jnp.int32, (1, LANES), 1)
    t_col = t0 + lax.broadcasted_iota(jnp.int32, (Q_TILE, 1), 0)
    t_rows = _tile_rows(t_col, NSA_HEADS)
    q_all = q_ref[0]

    gates = _dot(jnp.concatenate(_split_bf16(ga_ref[0]), axis=1), x_ref[...])

    j_col = lax.broadcasted_iota(jnp.int32, (N_SEL, 1), 0)
    t_lane = t0 + lax.broadcasted_iota(jnp.int32, (1, Q_TILE), 1)
    cur = t_lane // SEL_BLOCK
    forced = (j_col == 0) | (j_col == cur) | (j_col == cur - 1)
    valid = j_col * SEL_BLOCK <= t_lane

    w0 = pl.multiple_of(jnp.maximum(t0 - WINDOW, 0), Q_TILE)
    wlen = WINDOW + Q_TILE
    diff = t_col - (w0 + lax.broadcasted_iota(jnp.int32, (1, wlen), 1))
    win_bias = _tile_rows(jnp.where((diff >= 0) & (diff < WINDOW), 0.0, NEG), NSA_HEADS)
    kw = kw_ref[0, pl.ds(w0, wlen), :]
    vw = vw_ref[0, pl.ds(w0, wlen), :]

    qs = jnp.concatenate(
        [jnp.where((lane // HEAD_DIM) == g, q_all[:, h * LANES:(h + 1) * LANES], jnp.zeros((), BF16))
         for g in range(NSA_KV_HEADS) for h in range(NSA_GROUP)], axis=0)

    s = _dot_nt(qs, kc_ref[0])
    m_cmp = (CMP_STRIDE * lane + CMP_BLOCK - 1) <= t_rows
    p = _masked_softmax(s, m_cmp)
    o_cmp = _dot(p.astype(BF16), vc_ref[0])

    wrow = nrows // WIN_CHAINS
    win_bias = win_bias[:wrow]
    vw_aug = jnp.concatenate([vw, jnp.ones((wlen, LANES), BF16)], axis=1)
    sw = [_dot_nt(qs[c * wrow:(c + 1) * wrow], kw) + win_bias for c in range(WIN_CHAINS)]
    ew = [jnp.exp2(s_c - jnp.max(s_c, axis=-1, keepdims=True)).astype(BF16) for s_c in sw]
    ow = jnp.concatenate([_dot(e_c, vw_aug) for e_c in ew], axis=0)
    o_win = ow[:, :LANES] / jnp.maximum(ow[:, LANES:], 1e-30)

    drops = []
    for g in range(NSA_KV_HEADS):
        pg = p[g * hq:(g + 1) * hq]
        psum = pg[0:Q_TILE] + pg[Q_TILE:2 * Q_TILE] + pg[2 * Q_TILE:3 * Q_TILE] + pg[3 * Q_TILE:4 * Q_TILE]
        ps_hi, ps_lo = _split_bf16(psum)
        score = _dot_nt(ovt_ref[...], ps_hi) + _dot_nt(ovt_ref[...], ps_lo)
        score = jnp.where(forced, FORCE, jnp.where(valid, score, -FORCE))
        rank = jnp.zeros((N_SEL, Q_TILE), F32)
        for jp in range(N_SEL):
            row = score[jp:jp + 1, :]
            ge = jnp.where(row >= score, 1.0, 0.0)
            gt = jnp.where(row > score, 1.0, 0.0)
            rank = rank + jnp.where(j_col > jp, ge, gt)
        drop_t = jnp.where(rank < SEL_TOPK, 0.0, NEG)
        drop_t = jnp.concatenate([drop_t, jnp.zeros((LANES - N_SEL, Q_TILE), F32)], axis=0)
        drop = drop_t.T.astype(BF16)
        drops.append(_tile_rows(drop, NSA_GROUP))
    q_aug = jnp.concatenate([qs, jnp.concatenate(drops, axis=0)], axis=1)

    crow = nrows // SEL_CHAINS
    q_parts = [q_aug[c * crow:(c + 1) * crow] for c in range(SEL_CHAINS)]

    def sel_tile(k0, carry, mask, width=SEL_KEY_TILE):
        k_aug = jnp.concatenate([ks_ref[0, pl.ds(k0, width), :], et_ref[pl.ds(k0, width), :]], axis=1)
        v_aug = jnp.concatenate([vs_ref[0, pl.ds(k0, width), :], jnp.ones((width, LANES), BF16)], axis=1)
        sc = [_dot_nt(q, k_aug) for q in q_parts]
        if mask is not None:
            sc = [jnp.where(mask, s_c, NEG) for s_c in sc]
        m_new = [jnp.maximum(m_i, jnp.max(s_c, axis=-1, keepdims=True)) for s_c, (m_i, _) in zip(sc, carry)]
        e = [jnp.exp2(s_c - m_c).astype(BF16) for s_c, m_c in zip(sc, m_new)]
        return tuple((m_c, jnp.exp2(m_i - m_c) * acc + _dot(e_c, v_aug))
                     for m_c, e_c, (m_i, acc) in zip(m_new, e, carry))

    blocks_per_tile = SEL_KEY_TILE // Q_TILE
    for n_tiles in range(1, et_ref.shape[0] // SEL_KEY_TILE + 1):
        for n_blocks in range(1, blocks_per_tile + 1):
            @pl.when(qi == (n_tiles - 1) * blocks_per_tile + n_blocks - 1)
            def _():
                carry = ((jnp.full((crow, 1), NEG, F32), jnp.zeros((crow, 2 * LANES), F32)),) * SEL_CHAINS
                for kt in range(n_tiles - 1):
                    carry = sel_tile(kt * SEL_KEY_TILE, carry, None)
                k_diag, width = (n_tiles - 1) * SEL_KEY_TILE, n_blocks * Q_TILE
                causal = _tile_rows((k_diag + lax.broadcasted_iota(jnp.int32, (1, width), 1)) <= t_col,
                                    crow // Q_TILE)
                acc = jnp.concatenate([a for _, a in sel_tile(k_diag, carry, causal, width)], axis=0)
                osel_ref[...] = acc[:, :LANES] / jnp.maximum(acc[:, LANES:], 1e-30)
    o_sel = osel_ref[...]

    branches = (o_cmp, o_sel, o_win)
    for h in range(NSA_GROUP):
        mixed = []
        for g in range(NSA_KV_HEADS):
            r0 = (g * NSA_GROUP + h) * Q_TILE
            acc = None
            for r in range(3):
                term = gates[:, r * NSA_Q + h * LANES: r * NSA_Q + (h + 1) * LANES] * branches[r][r0:r0 + Q_TILE]
                acc = term if acc is None else acc + term
            mixed.append(acc)
        o_ref[0, :, h * LANES:(h + 1) * LANES] = jnp.where(lane < HEAD_DIM, mixed[0], mixed[1]).astype(o_ref.dtype)


def _nsa_attention(qa, kc, vc, ksel, vsel, kwin, vwin, ga, ovt, expand, gate_x):
    b, s, _ = qa.shape
    per_q = lambda w: pl.BlockSpec((1, Q_TILE, w), lambda i, j: (i, j, 0))
    per_b = lambda a: pl.BlockSpec((1,) + a.shape[1:], lambda i, j: (i, 0, 0))
    full = _resident
    return pl.pallas_call(
        _nsa_kernel,
        grid=(b, s // Q_TILE),
        in_specs=[per_q(NSA_Q), per_b(kc), per_b(vc), per_b(ksel), per_b(vsel), per_b(kwin), per_b(vwin),
                  per_q(LANES), full(ovt), full(expand), full(gate_x)],
        out_specs=per_q(NSA_Q),
        out_shape=jax.ShapeDtypeStruct((b, s, NSA_Q), BF16),
        scratch_shapes=[pltpu.VMEM((NSA_HEADS * Q_TILE, LANES), F32)],
        compiler_params=pltpu.CompilerParams(
            dimension_semantics=("parallel", "parallel"), vmem_limit_bytes=VMEM_LIMIT),
        name="nsa_attention",
    )(qa, kc, vc, ksel, vsel, kwin, vwin, ga, ovt, expand, gate_x)


SB_SKIP = 150.0
SB_PAIRS = 4


def _sb_kernel(q_ref, k_ref, v_ref, u_ref, o_ref, run_ref, acc_ref):
    qi = pl.program_id(2)
    t0 = qi * SB_TILE
    lane = lax.broadcasted_iota(jnp.int32, (1, LANES), 1)
    t_col = t0 + lax.broadcasted_iota(jnp.int32, (SB_TILE, 1), 0)
    diag_mask = (t0 + lax.broadcasted_iota(jnp.int32, (1, SB_TILE), 1)) < t_col
    diag_mask = _tile_rows(diag_mask, 2)
    qms = []
    for p in range(SB_PAIRS):
        q = q_ref[0, :, p * LANES:(p + 1) * LANES]
        qms.append(jnp.concatenate(
            [jnp.where((lane // HEAD_DIM) == hl, q, jnp.zeros((), BF16)) for hl in range(2)], axis=0))

    def tiles(k0, state, mask):
        pairs = range(SB_PAIRS)
        cols = [slice(p * LANES, (p + 1) * LANES) for p in pairs]
        z2 = [_dot_nt(qms[p], k_ref[0, pl.ds(k0, SB_TILE), cols[p]]) for p in pairs]
        split = []
        for p in pairs:
            neg_abs = pltpu.bitcast(pltpu.bitcast(z2[p], jnp.uint32) | jnp.uint32(0x80000000), F32)
            nfail = jnp.maximum(z2[p], 0.0) + jnp.log2(1.0 + jnp.exp2(neg_abs))
            if mask is not None:
                nfail = jnp.where(mask, nfail, 0.0)
            split.append(jnp.concatenate(_split_bf16(nfail), axis=1))
        tail = [_dot(split[p], u_ref[...]) for p in pairs]
        a = []
        for p in pairs:
            ap = jnp.exp2(z2[p] + tail[p] + state[p][0])
            if mask is not None:
                ap = jnp.where(mask, ap, 0.0)
            a.append(ap.astype(BF16))
        return tuple((state[p][0] + tail[p][:, 0:1],
                      state[p][1] + _dot(a[p], v_ref[0, pl.ds(k0, SB_TILE), cols[p]])) for p in pairs)

    zero = (jnp.zeros((2 * SB_TILE, 1), F32), jnp.zeros((2 * SB_TILE, LANES), F32))

    def save(state):
        for p, (run, acc) in enumerate(state):
            run_ref[p] = run
            acc_ref[p] = acc

    @pl.when(qi == 0)
    def _():
        save(tiles(0, (zero,) * SB_PAIRS, diag_mask))

    @pl.when(qi > 0)
    def _():
        state = tiles(pl.multiple_of(t0, SB_TILE), (zero,) * SB_PAIRS, diag_mask)
        save(tiles(pl.multiple_of(t0 - SB_TILE, SB_TILE), state, None))

    state = tuple((run_ref[p], acc_ref[p]) for p in range(SB_PAIRS))

    def cond(c):
        i, state = c
        top = functools.reduce(jnp.maximum, [run for run, _ in state])
        return jnp.logical_and(i <= qi, jnp.max(top) > -SB_SKIP)

    def body(c):
        i, state = c
        k0 = pl.multiple_of((qi - i) * SB_TILE, SB_TILE)
        return i + 1, tiles(k0, state, None)

    _, state = lax.while_loop(cond, body, (jnp.int32(2), state))
    for p, (_, acc) in enumerate(state):
        o_ref[0, :, p * LANES:(p + 1) * LANES] = jnp.where(
            lane < HEAD_DIM, acc[:SB_TILE], acc[SB_TILE:]).astype(o_ref.dtype)


def _sb_attention(qb, kb, vb, upper):
    b, s, w = qb.shape
    wb = SB_PAIRS * LANES
    return pl.pallas_call(
        _sb_kernel,
        grid=(b, w // wb, s // SB_TILE),
        in_specs=[pl.BlockSpec((1, SB_TILE, wb), lambda i, j, t: (i, t, j)),
                  pl.BlockSpec((1, s, wb), lambda i, j, t: (i, 0, j)),
                  pl.BlockSpec((1, s, wb), lambda i, j, t: (i, 0, j)),
                  _resident(upper)],
        out_specs=pl.BlockSpec((1, SB_TILE, wb), lambda i, j, t: (i, t, j)),
        out_shape=jax.ShapeDtypeStruct((b, s, w), BF16),
        scratch_shapes=[pltpu.VMEM((SB_PAIRS, 2 * SB_TILE, 1), F32), pltpu.VMEM((SB_PAIRS, 2 * SB_TILE, LANES), F32)],
        compiler_params=pltpu.CompilerParams(
            dimension_semantics=("parallel", "parallel", "parallel"), vmem_limit_bytes=VMEM_LIMIT),
        name="sb_attention",
    )(qb, kb, vb, upper)


MERGE_TM = 512
MERGE_CHAINS = 2


def _merge_kernel(oa_ref, ob_ref, xb_ref, x_ref, wg_ref, wa_ref, wb_ref, wo_ref, g_ref, b_ref, y_ref, ybf_ref):
    half = xb_ref.shape[0] // MERGE_CHAINS
    rows = [slice(c * half, (c + 1) * half) for c in range(MERGE_CHAINS)]
    gate = [_dot(xb_ref[r, :], wg_ref[...]) for r in rows]
    ya = [_dot(oa_ref[r, :], wa_ref[...]) for r in rows]
    yb = [_dot(ob_ref[r, :], wb_ref[...]) for r in rows]
    merged = [jax.nn.sigmoid(g_c[:, :D_MODEL]) * a_c + jax.nn.sigmoid(g_c[:, D_MODEL:]) * b_c
              for g_c, a_c, b_c in zip(gate, ya, yb)]
    y = [_dot(m_c.astype(BF16), wo_ref[...]) for m_c in merged]
    for r, y_c in zip(rows, y):
        out = _layer_norm(DEEPNORM_ALPHA * x_ref[r, :] + y_c, g_ref[...], b_ref[...])
        y_ref[r, :] = out
        ybf_ref[r, :] = out.astype(BF16)


def _merge(oa, ob, x_bf, x, layer, wg, wa, wb, wo, g, b):
    t = x.shape[0]
    tm = MERGE_TM
    row = lambda w: pl.BlockSpec((tm, w), lambda i: (i, 0))
    full = lambda a: _layer_block(a, layer)
    gate_cols = pl.BlockSpec((None, D_MODEL, _GATE_WIDTH), lambda i: (layer, 0, 0), pipeline_mode=pl.Buffered(1))
    return pl.pallas_call(
        _merge_kernel,
        grid=(t // tm,),
        in_specs=[row(NSA_Q), row(SB_W), row(D_MODEL), row(D_MODEL),
                  gate_cols, full(wa), full(wb), full(wo), full(g), full(b)],
        out_specs=[row(D_MODEL), row(D_MODEL)],
        out_shape=[jax.ShapeDtypeStruct((t, D_MODEL), F32), jax.ShapeDtypeStruct((t, D_MODEL), BF16)],
        compiler_params=pltpu.CompilerParams(
            dimension_semantics=("parallel",), vmem_limit_bytes=VMEM_LIMIT),
        name="merge_out_ln",
    )(oa, ob, x_bf, x, wg, wa, wb, wo, g, b)


FFN_TM = 512
FFN_CHUNK = 256
HALO = 16


def _ffn_kernel(h_ref, halo_ref, x_ref, wu_ref, cw_ref, cb_ref, wd_ref, g_ref, b_ref, y_ref, ybf_ref,
                hx_ref, act_ref, *, tiles_per_seq):
    i = pl.program_id(0)
    has_prev = (i % tiles_per_seq) != 0
    tm = h_ref.shape[0]
    n_chunks = D_FF // FFN_CHUNK
    hx_ref[0:HALO, :] = jnp.where(has_prev, halo_ref[...], jnp.zeros((), BF16))
    hx_ref[HALO:, :] = h_ref[...]

    def cols_of(half, c):
        return slice(half * D_FF + c * FFN_CHUNK, half * D_FF + (c + 1) * FFN_CHUNK)

    def up(c):
        return tuple(_dot(hx_ref[...], wu_ref[:, cols_of(half, c)]) for half in range(2))

    def gate(c, us):
        def conv(half):
            cols, u = cols_of(half, c), us[half]
            return (cw_ref[0:1, cols] * pltpu.roll(u, 2, 0)[HALO:] + cw_ref[1:2, cols] * pltpu.roll(u, 1, 0)[HALO:]
                    + cw_ref[2:3, cols] * u[HALO:] + cb_ref[:, cols])
        a = conv(0)
        act_ref[:, c * FFN_CHUNK:(c + 1) * FFN_CHUNK] = (a * jax.nn.sigmoid(a) * conv(1)).astype(BF16)

    us = up(0)
    for c in range(1, n_chunks):
        nxt = up(c)
        gate(c - 1, us)
        us = nxt
    gate(n_chunks - 1, us)
    y = _dot(act_ref[...], wd_ref[...])
    out = _layer_norm(DEEPNORM_ALPHA * x_ref[...] + y, g_ref[...], b_ref[...])
    y_ref[...] = out
    ybf_ref[...] = out.astype(BF16)


def _conv_ffn(h_bf, x, layer, w_up, conv_w, conv_b, w_down, g, b, seq):
    t = x.shape[0]
    tm = FFN_TM
    row = lambda w: pl.BlockSpec((tm, w), lambda i: (i, 0))
    return pl.pallas_call(
        functools.partial(_ffn_kernel, tiles_per_seq=seq // tm),
        grid=(t // tm,),
        in_specs=[row(D_MODEL),
                  pl.BlockSpec((HALO, D_MODEL), lambda i: (jnp.maximum(i * (tm // HALO) - 1, 0), 0)),
                  row(D_MODEL),
                  *[_layer_block(a, layer) for a in (w_up, conv_w, conv_b, w_down, g, b)]],
        out_specs=[row(D_MODEL), row(D_MODEL)],
        out_shape=[jax.ShapeDtypeStruct((t, D_MODEL), F32), jax.ShapeDtypeStruct((t, D_MODEL), BF16)],
        scratch_shapes=[pltpu.VMEM((tm + HALO, D_MODEL), BF16), pltpu.VMEM((tm, D_FF), BF16)],
        compiler_params=pltpu.CompilerParams(
            dimension_semantics=("parallel",), vmem_limit_bytes=VMEM_LIMIT),
        name="conv_ffn_ln",
    )(h_bf, h_bf, x, w_up, conv_w, conv_b, w_down, g, b)


def _rope_tables(seq):
    inv_freq = ROPE_THETA ** (-np.arange(0, ROT_DIM, 2, dtype=np.float32) / ROT_DIM)
    ang = jnp.arange(seq, dtype=F32)[:, None] * jnp.asarray(inv_freq, F32)[None, :]
    cos, sin = jnp.cos(ang), jnp.sin(ang)
    half = ROT_DIM // 2
    ones = jnp.ones((seq, HEAD_DIM - ROT_DIM), F32)
    zeros = jnp.zeros((seq, HEAD_DIM - ROT_DIM), F32)
    zh = jnp.zeros((seq, half), F32)
    cos_t = jnp.concatenate([cos, cos, ones], axis=1)
    sin_prev = jnp.concatenate([zh, sin, zeros], axis=1)
    sin_next = jnp.concatenate([-sin, zh, zeros], axis=1)
    rep = LANES // HEAD_DIM
    return tuple(jnp.tile(a, (1, rep)) for a in (cos_t, sin_prev, sin_next))


_QA_PERM = np.concatenate([np.concatenate([np.arange(h * HEAD_DIM, (h + 1) * HEAD_DIM),
                                           np.arange((NSA_GROUP + h) * HEAD_DIM, (NSA_GROUP + h + 1) * HEAD_DIM)])
                           for h in range(NSA_GROUP)])


def _selection_overlap_t(seq):
    n_cmp = (seq - CMP_BLOCK) // CMP_STRIDE + 1
    n_sel = seq // SEL_BLOCK
    cs = np.arange(n_cmp) * CMP_STRIDE
    ce = cs + CMP_BLOCK
    ss = np.arange(n_sel) * SEL_BLOCK
    se = ss + SEL_BLOCK
    ov = np.clip(np.minimum(ce[:, None], se[None, :]) - np.maximum(cs[:, None], ss[None, :]), 0, None) / CMP_BLOCK
    out = np.zeros((n_sel, seq // CMP_STRIDE), np.float32)
    out[:, :n_cmp] = ov.T
    return jnp.asarray(out, BF16)


def _block_indicator(seq):
    e = np.zeros((seq, LANES), np.float32)
    e[np.arange(seq), np.arange(seq) // SEL_BLOCK] = 1.0
    return jnp.asarray(e, BF16)


def _gate_expand():
    x = np.zeros((LANES, 3 * NSA_Q), np.float32)
    for col in range(NSA_Q):
        hh = _QA_PERM[col] // HEAD_DIM
        for r in range(3):
            x[hh * 3 + r, r * NSA_Q + col] = 1.0
    return jnp.asarray(np.concatenate([x, x], axis=0), BF16)


def _neg_later_keys(n):
    u = -(np.arange(n)[:, None] >= np.arange(n)[None, :]).astype(np.float32)
    return jnp.asarray(np.concatenate([u, u], axis=0), BF16)


def _heads_to_qa_order(a, axis):
    shape = a.shape
    a = a.reshape(shape[:axis] + (NSA_KV_HEADS, NSA_GROUP, HEAD_DIM) + shape[axis + 1:])
    return jnp.swapaxes(a, axis, axis + 1).reshape(shape)


def _prep_in_proj(w):
    q_a, kc, vc, ks, vs, kw, vw, g_a, q_b, k_b, v_b, g_m = jnp.split(w.astype(BF16), SPLIT_POINTS, axis=-1)
    g_a = jnp.pad(g_a, ((0, 0), (0, 0), (0, LANES - g_a.shape[-1])))
    cols = [g_m, _heads_to_qa_order(q_a, 2), kc, ks, kw, vc, vs, vw, q_b, k_b, v_b, g_a]
    return jnp.concatenate(cols, axis=-1)


def _prep_compress(pos, w1, b1, w2):
    half = CMP_BLOCK // 2
    lead = pos.shape[:2]
    pos_t = jnp.broadcast_to(pos.reshape(lead + (2, half, 1, HEAD_DIM)),
                             lead + (2, half, NSA_KV_HEADS, HEAD_DIM)).reshape(lead + (2, half * NSA_KV))
    eye = jnp.eye(NSA_KV_HEADS, dtype=w1.dtype)
    w1r = w1.reshape(lead + (2, half, 1, HEAD_DIM, 1, CMP_HIDDEN))
    w1x = (w1r * eye[:, None, :, None]).reshape(lead + (2, half * NSA_KV, NSA_KV_HEADS * CMP_HIDDEN))
    b1t = jnp.tile(b1.reshape(lead + (1, CMP_HIDDEN)), (1, 1, 1, NSA_KV_HEADS))
    w2x = (w2.reshape(lead + (1, CMP_HIDDEN, 1, HEAD_DIM)) * eye[:, None, :, None]).reshape(
        lead + (NSA_KV_HEADS * CMP_HIDDEN, NSA_KV))
    return pos_t, w1x.astype(BF16), b1t, w2x.astype(BF16)


def kernel(x, w_in, cmp_pos_k, cmp_w1_k, cmp_b1_k, cmp_w2_k, cmp_pos_v, cmp_w1_v, cmp_b1_v, cmp_w2_v,
           w_branch_a, w_branch_b, w_out, ln_mix_g, ln_mix_b, w_up, conv_w, conv_b, w_down,
           ln_ffn_g, ln_ffn_b):
    b, s, d = x.shape
    t = b * s
    assert d == D_MODEL and s // SEL_BLOCK == N_SEL and s // CMP_STRIDE == LANES
    cos_t, sin_prev, sin_next = _rope_tables(s)
    ovt = _selection_overlap_t(s)
    expand = _block_indicator(s)
    gate_x = _gate_expand()
    upper = _neg_later_keys(SB_TILE)

    w_proj = _prep_in_proj(w_in)
    wa = _heads_to_qa_order(w_branch_a.astype(BF16), 1)
    wb = w_branch_b.astype(BF16)
    wo = w_out.astype(BF16)
    wu = w_up.astype(BF16)
    wd = w_down.astype(BF16)
    cmp_params = _prep_compress(*(jnp.stack(kv) for kv in ((cmp_pos_k, cmp_pos_v), (cmp_w1_k, cmp_w1_v),
                                                           (cmp_b1_k, cmp_b1_v), (cmp_w2_k, cmp_w2_v))))
    stack_row = lambda a: a.reshape(a.shape[0], 1, a.shape[1])
    cb, g_mix, b_mix, g_ffn, b_ffn = map(stack_row, (conv_b, ln_mix_g, ln_mix_b, ln_ffn_g, ln_ffn_b))

    xf = x.reshape(t, d)
    xb = xf.astype(BF16)
    for l in range(DEPTH):
        outs = _in_proj(xb, w_proj, l, cos_t, sin_prev, sin_next, s)
        qa, kcmp, ksel, kwin, vcmp, vsel, vwin, qb, kb, vb, ga = outs
        rows = s // CMP_STRIDE
        kc = _compress(kcmp.reshape(b, rows, CMP_STRIDE * LANES), 0, l, *cmp_params)
        vc = _compress(vcmp.reshape(b, rows, CMP_STRIDE * LANES), 1, l, *cmp_params)
        r3 = lambda a: a.reshape(b, s, a.shape[-1])
        oa = _nsa_attention(r3(qa), kc, vc, r3(ksel), r3(vsel), r3(kwin), r3(vwin), r3(ga),
                            ovt, expand, gate_x)
        ob = _sb_attention(r3(qb), r3(kb), r3(vb), upper)
        xf, xb = _merge(oa.reshape(t, NSA_Q), ob.reshape(t, SB_W), xb, xf, l,
                        w_proj, wa, wb, wo, g_mix, b_mix)
        xf, xb = _conv_ffn(xb, xf, l, wu, conv_w, cb, wd, g_ffn, b_ffn, s)
    return xf.reshape(b, s, d)
```

```python
import functools

import numpy as np
import jax
import jax.numpy as jnp
from jax import lax
from jax.experimental import pallas as pl
from jax.experimental.pallas import tpu as pltpu

D_MODEL = 1024
DEPTH = 4
HEAD_DIM = 64
NSA_HEADS = 8
NSA_KV_HEADS = 2
NSA_GROUP = NSA_HEADS // NSA_KV_HEADS
SB_HEADS = 8
CMP_BLOCK = 32
CMP_STRIDE = 16
CMP_HIDDEN = 128
SEL_BLOCK = 64
SEL_TOPK = 16
WINDOW = 512
ROPE_THETA = 500000.0
ROT_DIM = HEAD_DIM // 4
D_FF = 2816
CONV_W = 3
LN_EPS = 1e-5
NEG = -1e30
FORCE = 1e4
DEEPNORM_ALPHA = (2.0 * DEPTH) ** 0.25

NSA_Q = NSA_HEADS * HEAD_DIM
NSA_KV = NSA_KV_HEADS * HEAD_DIM
SB_W = SB_HEADS * HEAD_DIM
SPLIT_SIZES = (NSA_Q, NSA_KV, NSA_KV, NSA_KV, NSA_KV, NSA_KV, NSA_KV, 3 * NSA_HEADS, SB_W, SB_W, SB_W, 2 * D_MODEL)
SPLIT_POINTS = tuple(int(v) for v in np.cumsum(SPLIT_SIZES)[:-1])

LANES = 128
VMEM_LIMIT = 56 * 1024 * 1024

BF16 = jnp.bfloat16
F32 = jnp.float32

Q_TILE = 128
SEL_KEY_TILE = 512
SB_TILE = 256
N_SEL = 32
SEL_CHAINS = 4
WIN_CHAINS = 4


def _dot(a, b):
    return jnp.dot(a, b, preferred_element_type=F32)


def _dot_nt(a, b):
    return lax.dot_general(a, b, (((1,), (1,)), ((), ())), preferred_element_type=F32)


def _split_bf16(x):
    hi = x.astype(BF16)
    lo = (x - hi.astype(F32)).astype(BF16)
    return hi, lo


def _layer_norm(z, g, b):
    mu = jnp.mean(z, axis=-1, keepdims=True)
    zc = z - mu
    var = jnp.mean(zc * zc, axis=-1, keepdims=True)
    return zc * lax.rsqrt(var + LN_EPS) * g + b


def _resident(a):
    return pl.BlockSpec(a.shape, lambda *_: (0,) * a.ndim, pipeline_mode=pl.Buffered(1))


def _layer_block(a, l):
    return pl.BlockSpec((None,) + a.shape[1:], lambda *_: (l,) + (0,) * (a.ndim - 1), pipeline_mode=pl.Buffered(1))


LOG2E = 1.4426950408889634
Q_SCALE = HEAD_DIM ** -0.5 * LOG2E
_PROJ_OUT = (
    ("qa", NSA_Q, BF16, True, False, Q_SCALE),
    ("kcmp", LANES, F32, True, False, 1.0),
    ("ksel", LANES, BF16, True, False, 1.0),
    ("kwin", LANES, BF16, True, False, 1.0),
    ("vcmp", LANES, F32, False, False, 1.0),
    ("vsel", LANES, BF16, False, False, 1.0),
    ("vwin", LANES, BF16, False, False, 1.0),
    ("qb", SB_W, BF16, False, False, Q_SCALE),
    ("kb", SB_W, BF16, False, False, 1.0),
    ("vb", SB_W, BF16, False, False, 1.0),
    ("ga", LANES, F32, False, True, 1.0),
)
_PROJ_WIDTH = sum(o[1] for o in _PROJ_OUT)
_PROJ_RUNS = (7, 4)
_ROWS16 = ("kcmp", "vcmp")
_GATE_WIDTH = 2 * D_MODEL
PROJ_TM = 512


def _rope(acc, cos_t, sin_prev, sin_next):
    chunks = []
    for c in range(acc.shape[1] // LANES):
        xc = acc[:, c * LANES:(c + 1) * LANES]
        chunks.append(xc * cos_t + pltpu.roll(xc, ROT_DIM // 2, 1) * sin_prev
                      + pltpu.roll(xc, LANES - ROT_DIM // 2, 1) * sin_next)
    return chunks[0] if len(chunks) == 1 else jnp.concatenate(chunks, axis=1)


def _proj_kernel(x_ref, w_ref, cos_ref, sp_ref, sn_ref, *refs):
    out_refs, stage_ref = refs[:-1], refs[-1]
    x = x_ref[...]
    tm = x.shape[0]

    def finish(acc, spec, o_ref):
        name, _, dtype, rotary, sigmoid, scale = spec
        if rotary:
            acc = _rope(acc, cos_ref[...], sp_ref[...], sn_ref[...])
        if sigmoid:
            acc = jax.nn.sigmoid(acc)
        if scale != 1.0:
            acc = acc * scale
        if name in _ROWS16:
            stage_ref[...] = acc
            for l in range(CMP_STRIDE):
                o_ref[:, l * LANES:(l + 1) * LANES] = stage_ref[pl.ds(l, tm // CMP_STRIDE, stride=CMP_STRIDE), :]
        else:
            o_ref[...] = acc.astype(dtype)

    col = _GATE_WIDTH
    pending = []
    start = 0
    for n_groups in _PROJ_RUNS:
        specs = _PROJ_OUT[start:start + n_groups]
        width = sum(s[1] for s in specs)
        acc = _dot(x, w_ref[:, col:col + width])
        for item in pending:
            finish(*item)
        pending, off = [], 0
        for spec, o_ref in zip(specs, out_refs[start:start + n_groups]):
            pending.append((acc[:, off:off + spec[1]], spec, o_ref))
            off += spec[1]
        col += width
        start += n_groups
    for item in pending:
        finish(*item)


def _in_proj(x_bf, w, layer, cos_t, sin_prev, sin_next, seq):
    t = x_bf.shape[0]
    tm = PROJ_TM
    tiles_per_seq = seq // tm
    tab_spec = pl.BlockSpec((tm, LANES), lambda i: (i % tiles_per_seq, 0))
    fold = lambda o: CMP_STRIDE if o[0] in _ROWS16 else 1
    return pl.pallas_call(
        _proj_kernel,
        grid=(t // tm,),
        in_specs=[
            pl.BlockSpec((tm, D_MODEL), lambda i: (i, 0)),
            _layer_block(w, layer),
            tab_spec, tab_spec, tab_spec,
        ],
        out_specs=[pl.BlockSpec((tm // fold(o), o[1] * fold(o)), lambda i: (i, 0)) for o in _PROJ_OUT],
        out_shape=[jax.ShapeDtypeStruct((t // fold(o), o[1] * fold(o)), o[2]) for o in _PROJ_OUT],
        scratch_shapes=[pltpu.VMEM((tm, LANES), F32)],
        compiler_params=pltpu.CompilerParams(
            dimension_semantics=("parallel",), vmem_limit_bytes=VMEM_LIMIT),
        name="in_proj",
    )(x_bf, w, cos_t, sin_prev, sin_next)


def _gelu_tanh(x):
    return 0.5 * x * (1.0 + jnp.tanh(np.sqrt(2.0 / np.pi).astype(np.float32) * (x + 0.044715 * (x * x * x))))


def _compress_kernel(r_ref, pos_ref, w1_ref, b1_ref, w2_ref, o_ref):
    r = r_ref[0]
    ra = (r + pos_ref[0:1, :]).astype(BF16)
    rb = (r + pos_ref[1:2, :]).astype(BF16)
    p1 = _dot(ra, w1_ref[0])
    p2 = _dot(rb, w1_ref[1])
    nrow = p2.shape[0]
    hid = _gelu_tanh(p1 + pltpu.roll(p2, nrow - 1, 0) + b1_ref[...])
    o_ref[0] = _dot(hid.astype(BF16), w2_ref[...]).astype(o_ref.dtype)


def _compress(r, which, layer, pos, w1, b1, w2):
    b, nrow, width = r.shape
    full = lambda a: pl.BlockSpec((None, None) + a.shape[2:], lambda i: (which, layer) + (0,) * (a.ndim - 2),
                                  pipeline_mode=pl.Buffered(1))
    return pl.pallas_call(
        _compress_kernel,
        grid=(b,),
        in_specs=[pl.BlockSpec((1, nrow, width), lambda i: (i, 0, 0)),
                  full(pos), full(w1), full(b1), full(w2)],
        out_specs=pl.BlockSpec((1, nrow, LANES), lambda i: (i, 0, 0)),
        out_shape=jax.ShapeDtypeStruct((b, nrow, LANES), BF16),
        compiler_params=pltpu.CompilerParams(
            dimension_semantics=("parallel",), vmem_limit_bytes=VMEM_LIMIT),
        name="compress",
    )(r, pos, w1, b1, w2)


def _masked_softmax(s, mask):
    s = jnp.where(mask, s, NEG)
    m = jnp.max(s, axis=-1, keepdims=True)
    e = jnp.where(mask, jnp.exp2(s - m), 0.0)
    return e / jnp.maximum(jnp.sum(e, axis=-1, keepdims=True), 1e-30)


def _tile_rows(a, n):
    return jnp.concatenate([a] * n, axis=0)


def _nsa_kernel(q_ref, kc_ref, vc_ref, ks_ref, vs_ref, kw_ref, vw_ref, ga_ref,
                ovt_ref, et_ref, x_ref, o_ref, osel_ref):
    qi = pl.program_id(1)
    t0 = qi * Q_TILE
    hq = NSA_GROUP * Q_TILE
    nrows = NSA_HEADS * Q_TILE
    lane = lax.broadcasted_iota(jnp.int32, (1, LANES), 1)
    t_col = t0 + lax.broadcasted_iota(jnp.int32, (Q_TILE, 1), 0)
    t_rows = _tile_rows(t_col, NSA_HEADS)
    q_all = q_ref[0]

    gates = _dot(jnp.concatenate(_split_bf16(ga_ref[0]), axis=1), x_ref[...])

    j_col = lax.broadcasted_iota(jnp.int32, (N_SEL, 1), 0)
    t_lane = t0 + lax.broadcasted_iota(jnp.int32, (1, Q_TILE), 1)
    cur = t_lane // SEL_BLOCK
    forced = (j_col == 0) | (j_col == cur) | (j_col == cur - 1)
    valid = j_col * SEL_BLOCK <= t_lane

    w0 = pl.multiple_of(jnp.maximum(t0 - WINDOW, 0), Q_TILE)
    wlen = WINDOW + Q_TILE
    diff = t_col - (w0 + lax.broadcasted_iota(jnp.int32, (1, wlen), 1))
    win_bias = _tile_rows(jnp.where((diff >= 0) & (diff < WINDOW), 0.0, NEG), NSA_HEADS)
    kw = kw_ref[0, pl.ds(w0, wlen), :]
    vw = vw_ref[0, pl.ds(w0, wlen), :]

    qs = jnp.concatenate(
        [jnp.where((lane // HEAD_DIM) == g, q_all[:, h * LANES:(h + 1) * LANES], jnp.zeros((), BF16))
         for g in range(NSA_KV_HEADS) for h in range(NSA_GROUP)], axis=0)

    s = _dot_nt(qs, kc_ref[0])
    m_cmp = (CMP_STRIDE * lane + CMP_BLOCK - 1) <= t_rows
    p = _masked_softmax(s, m_cmp)
    o_cmp = _dot(p.astype(BF16), vc_ref[0])

    wrow = nrows // WIN_CHAINS
    win_bias = win_bias[:wrow]
    vw_aug = jnp.concatenate([vw, jnp.ones((wlen, LANES), BF16)], axis=1)
    sw = [_dot_nt(qs[c * wrow:(c + 1) * wrow], kw) + win_bias for c in range(WIN_CHAINS)]
    ew = [jnp.exp2(s_c - jnp.max(s_c, axis=-1, keepdims=True)).astype(BF16) for s_c in sw]
    ow = jnp.concatenate([_dot(e_c, vw_aug) for e_c in ew], axis=0)
    o_win = ow[:, :LANES] / jnp.maximum(ow[:, LANES:], 1e-30)

    drops = []
    for g in range(NSA_KV_HEADS):
        pg = p[g * hq:(g + 1) * hq]
        psum = pg[0:Q_TILE] + pg[Q_TILE:2 * Q_TILE] + pg[2 * Q_TILE:3 * Q_TILE] + pg[3 * Q_TILE:4 * Q_TILE]
        ps_hi, ps_lo = _split_bf16(psum)
        score = _dot_nt(ovt_ref[...], ps_hi) + _dot_nt(ovt_ref[...], ps_lo)
        score = jnp.where(forced, FORCE, jnp.where(valid, score, -FORCE))
        rank = jnp.zeros((N_SEL, Q_TILE), F32)
        for jp in range(N_SEL):
            row = score[jp:jp + 1, :]
            ge = jnp.where(row >= score, 1.0, 0.0)
            gt = jnp.where(row > score, 1.0, 0.0)
            rank = rank + jnp.where(j_col > jp, ge, gt)
        drop_t = jnp.where(rank < SEL_TOPK, 0.0, NEG)
        drop_t = jnp.concatenate([drop_t, jnp.zeros((LANES - N_SEL, Q_TILE), F32)], axis=0)
        drop = drop_t.T.astype(BF16)
        drops.append(_tile_rows(drop, NSA_GROUP))
    q_aug = jnp.concatenate([qs, jnp.concatenate(drops, axis=0)], axis=1)

    crow = nrows // SEL_CHAINS
    q_parts = [q_aug[c * crow:(c + 1) * crow] for c in range(SEL_CHAINS)]

    def sel_tile(k0, carry, mask, width=SEL_KEY_TILE):
        k_aug = jnp.concatenate([ks_ref[0, pl.ds(k0, width), :], et_ref[pl.ds(k0, width), :]], axis=1)
        v_aug = jnp.concatenate([vs_ref[0, pl.ds(k0, width), :], jnp.ones((width, LANES), BF16)], axis=1)
        sc = [_dot_nt(q, k_aug) for q in q_parts]
        if mask is not None:
            sc = [jnp.where(mask, s_c, NEG) for s_c in sc]
        m_new = [jnp.maximum(m_i, jnp.max(s_c, axis=-1, keepdims=True)) for s_c, (m_i, _) in zip(sc, carry)]
        e = [jnp.exp2(s_c - m_c).astype(BF16) for s_c, m_c in zip(sc, m_new)]
        return tuple((m_c, jnp.exp2(m_i - m_c) * acc + _dot(e_c, v_aug))
                     for m_c, e_c, (m_i, acc) in zip(m_new, e, carry))

    blocks_per_tile = SEL_KEY_TILE // Q_TILE
    for n_tiles in range(1, et_ref.shape[0] // SEL_KEY_TILE + 1):
        for n_blocks in range(1, blocks_per_tile + 1):
            @pl.when(qi == (n_tiles - 1) * blocks_per_tile + n_blocks - 1)
            def _():
                carry = ((jnp.full((crow, 1), NEG, F32), jnp.zeros((crow, 2 * LANES), F32)),) * SEL_CHAINS
                for kt in range(n_tiles - 1):
                    carry = sel_tile(kt * SEL_KEY_TILE, carry, None)
                k_diag, width = (n_tiles - 1) * SEL_KEY_TILE, n_blocks * Q_TILE
                causal = _tile_rows((k_diag + lax.broadcasted_iota(jnp.int32, (1, width), 1)) <= t_col,
                                    crow // Q_TILE)
                acc = jnp.concatenate([a for _, a in sel_tile(k_diag, carry, causal, width)], axis=0)
                osel_ref[...] = acc[:, :LANES] / jnp.maximum(acc[:, LANES:], 1e-30)
    o_sel = osel_ref[...]

    branches = (o_cmp, o_sel, o_win)
    for h in range(NSA_GROUP):
        mixed = []
        for g in range(NSA_KV_HEADS):
            r0 = (g * NSA_GROUP + h) * Q_TILE
            acc = None
            for r in range(3):
                term = gates[:, r * NSA_Q + h * LANES: r * NSA_Q + (h + 1) * LANES] * branches[r][r0:r0 + Q_TILE]
                acc = term if acc is None else acc + term
            mixed.append(acc)
        o_ref[0, :, h * LANES:(h + 1) * LANES] = jnp.where(lane < HEAD_DIM, mixed[0], mixed[1]).astype(o_ref.dtype)


def _nsa_attention(qa, kc, vc, ksel, vsel, kwin, vwin, ga, ovt, expand, gate_x):
    b, s, _ = qa.shape
    per_q = lambda w: pl.BlockSpec((1, Q_TILE, w), lambda i, j: (i, j, 0))
    per_b = lambda a: pl.BlockSpec((1,) + a.shape[1:], lambda i, j: (i, 0, 0))
    full = _resident
    return pl.pallas_call(
        _nsa_kernel,
        grid=(b, s // Q_TILE),
        in_specs=[per_q(NSA_Q), per_b(kc), per_b(vc), per_b(ksel), per_b(vsel), per_b(kwin), per_b(vwin),
                  per_q(LANES), full(ovt), full(expand), full(gate_x)],
        out_specs=per_q(NSA_Q),
        out_shape=jax.ShapeDtypeStruct((b, s, NSA_Q), BF16),
        scratch_shapes=[pltpu.VMEM((NSA_HEADS * Q_TILE, LANES), F32)],
        compiler_params=pltpu.CompilerParams(
            dimension_semantics=("parallel", "parallel"), vmem_limit_bytes=VMEM_LIMIT),
        name="nsa_attention",
    )(qa, kc, vc, ksel, vsel, kwin, vwin, ga, ovt, expand, gate_x)


SB_SKIP = 150.0
SB_PAIRS = 4


def _sb_kernel(q_ref, k_ref, v_ref, u_ref, o_ref, run_ref, acc_ref):
    qi = pl.program_id(2)
    t0 = qi * SB_TILE
    lane = lax.broadcasted_iota(jnp.int32, (1, LANES), 1)
    t_col = t0 + lax.broadcasted_iota(jnp.int32, (SB_TILE, 1), 0)
    diag_mask = (t0 + lax.broadcasted_iota(jnp.int32, (1, SB_TILE), 1)) < t_col
    diag_mask = _tile_rows(diag_mask, 2)
    qms = []
    for p in range(SB_PAIRS):
        q = q_ref[0, :, p * LANES:(p + 1) * LANES]
        qms.append(jnp.concatenate(
            [jnp.where((lane // HEAD_DIM) == hl, q, jnp.zeros((), BF16)) for hl in range(2)], axis=0))

    def tiles(k0, state, mask):
        chains = [(p, hl) for p in range(SB_PAIRS) for hl in range(2)]
        cols = lambda p: slice(p * LANES, (p + 1) * LANES)
        rows = lambda hl: slice(hl * SB_TILE, (hl + 1) * SB_TILE)
        hmask = None if mask is None else mask[:SB_TILE]
        z2 = [_dot_nt(qms[p][rows(hl)], k_ref[0, pl.ds(k0, SB_TILE), cols(p)]) for p, hl in chains]
        split = []
        for z_c in z2:
            neg_abs = pltpu.bitcast(pltpu.bitcast(z_c, jnp.uint32) | jnp.uint32(0x80000000), F32)
            nfail = jnp.maximum(z_c, 0.0) + jnp.log2(1.0 + jnp.exp2(neg_abs))
            if hmask is not None:
                nfail = jnp.where(hmask, nfail, 0.0)
            split.append(jnp.concatenate(_split_bf16(nfail), axis=1))
        tail = [_dot(s_c, u_ref[...]) for s_c in split]
        a = []
        for (p, hl), z_c, t_c in zip(chains, z2, tail):
            ap = jnp.exp2(z_c + t_c + state[p][0][rows(hl)])
            if hmask is not None:
                ap = jnp.where(hmask, ap, 0.0)
            a.append(ap.astype(BF16))
        pv = [_dot(a_c, v_ref[0, pl.ds(k0, SB_TILE), cols(p)]) for (p, hl), a_c in zip(chains, a)]
        return tuple((state[p][0] + jnp.concatenate([tail[2 * p][:, 0:1], tail[2 * p + 1][:, 0:1]], axis=0),
                      state[p][1] + jnp.concatenate([pv[2 * p], pv[2 * p + 1]], axis=0)) for p in range(SB_PAIRS))

    zero = (jnp.zeros((2 * SB_TILE, 1), F32), jnp.zeros((2 * SB_TILE, LANES), F32))

    def save(state):
        for p, (run, acc) in enumerate(state):
            run_ref[p] = run
            acc_ref[p] = acc

    @pl.when(qi == 0)
    def _():
        save(tiles(0, (zero,) * SB_PAIRS, diag_mask))

    @pl.when(qi > 0)
    def _():
        state = tiles(pl.multiple_of(t0, SB_TILE), (zero,) * SB_PAIRS, diag_mask)
        save(tiles(pl.multiple_of(t0 - SB_TILE, SB_TILE), state, None))

    state = tuple((run_ref[p], acc_ref[p]) for p in range(SB_PAIRS))

    def cond(c):
        i, state = c
        top = functools.reduce(jnp.maximum, [run for run, _ in state])
        return jnp.logical_and(i <= qi, jnp.max(top) > -SB_SKIP)

    def body(c):
        i, state = c
        k0 = pl.multiple_of((qi - i) * SB_TILE, SB_TILE)
        return i + 1, tiles(k0, state, None)

    _, state = lax.while_loop(cond, body, (jnp.int32(2), state))
    for p, (_, acc) in enumerate(state):
        o_ref[0, :, p * LANES:(p + 1) * LANES] = jnp.where(
            lane < HEAD_DIM, acc[:SB_TILE], acc[SB_TILE:]).astype(o_ref.dtype)


def _sb_attention(qb, kb, vb, upper):
    b, s, w = qb.shape
    wb = SB_PAIRS * LANES
    return pl.pallas_call(
        _sb_kernel,
        grid=(b, w // wb, s // SB_TILE),
        in_specs=[pl.BlockSpec((1, SB_TILE, wb), lambda i, j, t: (i, t, j)),
                  pl.BlockSpec((1, s, wb), lambda i, j, t: (i, 0, j)),
                  pl.BlockSpec((1, s, wb), lambda i, j, t: (i, 0, j)),
                  _resident(upper)],
        out_specs=pl.BlockSpec((1, SB_TILE, wb), lambda i, j, t: (i, t, j)),
        out_shape=jax.ShapeDtypeStruct((b, s, w), BF16),
        scratch_shapes=[pltpu.VMEM((SB_PAIRS, 2 * SB_TILE, 1), F32), pltpu.VMEM((SB_PAIRS, 2 * SB_TILE, LANES), F32)],
        compiler_params=pltpu.CompilerParams(
            dimension_semantics=("parallel", "parallel", "parallel"), vmem_limit_bytes=VMEM_LIMIT),
        name="sb_attention",
    )(qb, kb, vb, upper)


MERGE_TM = 512
MERGE_CHAINS = 2


def _merge_kernel(oa_ref, ob_ref, xb_ref, x_ref, wg_ref, wa_ref, wb_ref, wo_ref, g_ref, b_ref, y_ref, ybf_ref):
    half = xb_ref.shape[0] // MERGE_CHAINS
    rows = [slice(c * half, (c + 1) * half) for c in range(MERGE_CHAINS)]
    gate = [_dot(xb_ref[r, :], wg_ref[...]) for r in rows]
    ya = [_dot(oa_ref[r, :], wa_ref[...]) for r in rows]
    yb = [_dot(ob_ref[r, :], wb_ref[...]) for r in rows]
    merged = [jax.nn.sigmoid(g_c[:, :D_MODEL]) * a_c + jax.nn.sigmoid(g_c[:, D_MODEL:]) * b_c
              for g_c, a_c, b_c in zip(gate, ya, yb)]
    y = [_dot(m_c.astype(BF16), wo_ref[...]) for m_c in merged]
    for r, y_c in zip(rows, y):
        out = _layer_norm(DEEPNORM_ALPHA * x_ref[r, :] + y_c, g_ref[...], b_ref[...])
        y_ref[r, :] = out
        ybf_ref[r, :] = out.astype(BF16)


def _merge(oa, ob, x_bf, x, layer, wg, wa, wb, wo, g, b):
    t = x.shape[0]
    tm = MERGE_TM
    row = lambda w: pl.BlockSpec((tm, w), lambda i: (i, 0))
    full = lambda a: _layer_block(a, layer)
    gate_cols = pl.BlockSpec((None, D_MODEL, _GATE_WIDTH), lambda i: (layer, 0, 0), pipeline_mode=pl.Buffered(1))
    return pl.pallas_call(
        _merge_kernel,
        grid=(t // tm,),
        in_specs=[row(NSA_Q), row(SB_W), row(D_MODEL), row(D_MODEL),
                  gate_cols, full(wa), full(wb), full(wo), full(g), full(b)],
        out_specs=[row(D_MODEL), row(D_MODEL)],
        out_shape=[jax.ShapeDtypeStruct((t, D_MODEL), F32), jax.ShapeDtypeStruct((t, D_MODEL), BF16)],
        compiler_params=pltpu.CompilerParams(
            dimension_semantics=("parallel",), vmem_limit_bytes=VMEM_LIMIT),
        name="merge_out_ln",
    )(oa, ob, x_bf, x, wg, wa, wb, wo, g, b)


FFN_TM = 512
FFN_CHUNK = 256
HALO = 16


def _ffn_kernel(h_ref, halo_ref, x_ref, wu_ref, cw_ref, cb_ref, wd_ref, g_ref, b_ref, y_ref, ybf_ref,
                hx_ref, act_ref, *, tiles_per_seq):
    i = pl.program_id(0)
    has_prev = (i % tiles_per_seq) != 0
    tm = h_ref.shape[0]
    n_chunks = D_FF // FFN_CHUNK
    hx_ref[0:HALO, :] = jnp.where(has_prev, halo_ref[...], jnp.zeros((), BF16))
    hx_ref[HALO:, :] = h_ref[...]

    def cols_of(half, c):
        return slice(half * D_FF + c * FFN_CHUNK, half * D_FF + (c + 1) * FFN_CHUNK)

    def up(c):
        return tuple(_dot(hx_ref[...], wu_ref[:, cols_of(half, c)]) for half in range(2))

    def gate(c, us):
        def conv(half):
            cols, u = cols_of(half, c), us[half]
            return (cw_ref[0:1, cols] * pltpu.roll(u, 2, 0)[HALO:] + cw_ref[1:2, cols] * pltpu.roll(u, 1, 0)[HALO:]
                    + cw_ref[2:3, cols] * u[HALO:] + cb_ref[:, cols])
        a = conv(0)
        act_ref[:, c * FFN_CHUNK:(c + 1) * FFN_CHUNK] = (a * jax.nn.sigmoid(a) * conv(1)).astype(BF16)

    us = up(0)
    for c in range(1, n_chunks):
        nxt = up(c)
        gate(c - 1, us)
        us = nxt
    gate(n_chunks - 1, us)
    y = _dot(act_ref[...], wd_ref[...])
    out = _layer_norm(DEEPNORM_ALPHA * x_ref[...] + y, g_ref[...], b_ref[...])
    y_ref[...] = out
    ybf_ref[...] = out.astype(BF16)


def _conv_ffn(h_bf, x, layer, w_up, conv_w, conv_b, w_down, g, b, seq):
    t = x.shape[0]
    tm = FFN_TM
    row = lambda w: pl.BlockSpec((tm, w), lambda i: (i, 0))
    return pl.pallas_call(
        functools.partial(_ffn_kernel, tiles_per_seq=seq // tm),
        grid=(t // tm,),
        in_specs=[row(D_MODEL),
                  pl.BlockSpec((HALO, D_MODEL), lambda i: (jnp.maximum(i * (tm // HALO) - 1, 0), 0)),
                  row(D_MODEL),
                  *[_layer_block(a, layer) for a in (w_up, conv_w, conv_b, w_down, g, b)]],
        out_specs=[row(D_MODEL), row(D_MODEL)],
        out_shape=[jax.ShapeDtypeStruct((t, D_MODEL), F32), jax.ShapeDtypeStruct((t, D_MODEL), BF16)],
        scratch_shapes=[pltpu.VMEM((tm + HALO, D_MODEL), BF16), pltpu.VMEM((tm, D_FF), BF16)],
        compiler_params=pltpu.CompilerParams(
            dimension_semantics=("parallel",), vmem_limit_bytes=VMEM_LIMIT),
        name="conv_ffn_ln",
    )(h_bf, h_bf, x, w_up, conv_w, conv_b, w_down, g, b)


def _rope_tables(seq):
    inv_freq = ROPE_THETA ** (-np.arange(0, ROT_DIM, 2, dtype=np.float32) / ROT_DIM)
    ang = jnp.arange(seq, dtype=F32)[:, None] * jnp.asarray(inv_freq, F32)[None, :]
    cos, sin = jnp.cos(ang), jnp.sin(ang)
    half = ROT_DIM // 2
    ones = jnp.ones((seq, HEAD_DIM - ROT_DIM), F32)
    zeros = jnp.zeros((seq, HEAD_DIM - ROT_DIM), F32)
    zh = jnp.zeros((seq, half), F32)
    cos_t = jnp.concatenate([cos, cos, ones], axis=1)
    sin_prev = jnp.concatenate([zh, sin, zeros], axis=1)
    sin_next = jnp.concatenate([-sin, zh, zeros], axis=1)
    rep = LANES // HEAD_DIM
    return tuple(jnp.tile(a, (1, rep)) for a in (cos_t, sin_prev, sin_next))


_QA_PERM = np.concatenate([np.concatenate([np.arange(h * HEAD_DIM, (h + 1) * HEAD_DIM),
                                           np.arange((NSA_GROUP + h) * HEAD_DIM, (NSA_GROUP + h + 1) * HEAD_DIM)])
                           for h in range(NSA_GROUP)])


def _selection_overlap_t(seq):
    n_cmp = (seq - CMP_BLOCK) // CMP_STRIDE + 1
    n_sel = seq // SEL_BLOCK
    cs = np.arange(n_cmp) * CMP_STRIDE
    ce = cs + CMP_BLOCK
    ss = np.arange(n_sel) * SEL_BLOCK
    se = ss + SEL_BLOCK
    ov = np.clip(np.minimum(ce[:, None], se[None, :]) - np.maximum(cs[:, None], ss[None, :]), 0, None) / CMP_BLOCK
    out = np.zeros((n_sel, seq // CMP_STRIDE), np.float32)
    out[:, :n_cmp] = ov.T
    return jnp.asarray(out, BF16)


def _block_indicator(seq):
    e = np.zeros((seq, LANES), np.float32)
    e[np.arange(seq), np.arange(seq) // SEL_BLOCK] = 1.0
    return jnp.asarray(e, BF16)


def _gate_expand():
    x = np.zeros((LANES, 3 * NSA_Q), np.float32)
    for col in range(NSA_Q):
        hh = _QA_PERM[col] // HEAD_DIM
        for r in range(3):
            x[hh * 3 + r, r * NSA_Q + col] = 1.0
    return jnp.asarray(np.concatenate([x, x], axis=0), BF16)


def _neg_later_keys(n):
    u = -(np.arange(n)[:, None] >= np.arange(n)[None, :]).astype(np.float32)
    return jnp.asarray(np.concatenate([u, u], axis=0), BF16)


def _heads_to_qa_order(a, axis):
    shape = a.shape
    a = a.reshape(shape[:axis] + (NSA_KV_HEADS, NSA_GROUP, HEAD_DIM) + shape[axis + 1:])
    return jnp.swapaxes(a, axis, axis + 1).reshape(shape)


def _prep_in_proj(w):
    q_a, kc, vc, ks, vs, kw, vw, g_a, q_b, k_b, v_b, g_m = jnp.split(w.astype(BF16), SPLIT_POINTS, axis=-1)
    g_a = jnp.pad(g_a, ((0, 0), (0, 0), (0, LANES - g_a.shape[-1])))
    cols = [g_m, _heads_to_qa_order(q_a, 2), kc, ks, kw, vc, vs, vw, q_b, k_b, v_b, g_a]
    return jnp.concatenate(cols, axis=-1)


def _prep_compress(pos, w1, b1, w2):
    half = CMP_BLOCK // 2
    lead = pos.shape[:2]
    pos_t = jnp.broadcast_to(pos.reshape(lead + (2, half, 1, HEAD_DIM)),
                             lead + (2, half, NSA_KV_HEADS, HEAD_DIM)).reshape(lead + (2, half * NSA_KV))
    eye = jnp.eye(NSA_KV_HEADS, dtype=w1.dtype)
    w1r = w1.reshape(lead + (2, half, 1, HEAD_DIM, 1, CMP_HIDDEN))
    w1x = (w1r * eye[:, None, :, None]).reshape(lead + (2, half * NSA_KV, NSA_KV_HEADS * CMP_HIDDEN))
    b1t = jnp.tile(b1.reshape(lead + (1, CMP_HIDDEN)), (1, 1, 1, NSA_KV_HEADS))
    w2x = (w2.reshape(lead + (1, CMP_HIDDEN, 1, HEAD_DIM)) * eye[:, None, :, None]).reshape(
        lead + (NSA_KV_HEADS * CMP_HIDDEN, NSA_KV))
    return pos_t, w1x.astype(BF16), b1t, w2x.astype(BF16)


def kernel(x, w_in, cmp_pos_k, cmp_w1_k, cmp_b1_k, cmp_w2_k, cmp_pos_v, cmp_w1_v, cmp_b1_v, cmp_w2_v,
           w_branch_a, w_branch_b, w_out, ln_mix_g, ln_mix_b, w_up, conv_w, conv_b, w_down,
           ln_ffn_g, ln_ffn_b):
    b, s, d = x.shape
    t = b * s
    assert d == D_MODEL and s // SEL_BLOCK == N_SEL and s // CMP_STRIDE == LANES
    cos_t, sin_prev, sin_next = _rope_tables(s)
    ovt = _selection_overlap_t(s)
    expand = _block_indicator(s)
    gate_x = _gate_expand()
    upper = _neg_later_keys(SB_TILE)

    w_proj = _prep_in_proj(w_in)
    wa = _heads_to_qa_order(w_branch_a.astype(BF16), 1)
    wb = w_branch_b.astype(BF16)
    wo = w_out.astype(BF16)
    wu = w_up.astype(BF16)
    wd = w_down.astype(BF16)
    cmp_params = _prep_compress(*(jnp.stack(kv) for kv in ((cmp_pos_k, cmp_pos_v), (cmp_w1_k, cmp_w1_v),
                                                           (cmp_b1_k, cmp_b1_v), (cmp_w2_k, cmp_w2_v))))
    stack_row = lambda a: a.reshape(a.shape[0], 1, a.shape[1])
    cb, g_mix, b_mix, g_ffn, b_ffn = map(stack_row, (conv_b, ln_mix_g, ln_mix_b, ln_ffn_g, ln_ffn_b))

    xf = x.reshape(t, d)
    xb = xf.astype(BF16)
    for l in range(DEPTH):
        outs = _in_proj(xb, w_proj, l, cos_t, sin_prev, sin_next, s)
        qa, kcmp, ksel, kwin, vcmp, vsel, vwin, qb, kb, vb, ga = outs
        rows = s // CMP_STRIDE
        kc = _compress(kcmp.reshape(b, rows, CMP_STRIDE * LANES), 0, l, *cmp_params)
        vc = _compress(vcmp.reshape(b, rows, CMP_STRIDE * LANES), 1, l, *cmp_params)
        r3 = lambda a: a.reshape(b, s, a.shape[-1])
        oa = _nsa_attention(r3(qa), kc, vc, r3(ksel), r3(vsel), r3(kwin), r3(vwin), r3(ga),
                            ovt, expand, gate_x)
        ob = _sb_attention(r3(qb), r3(kb), r3(vb), upper)
        xf, xb = _merge(oa.reshape(t, NSA_Q), ob.reshape(t, SB_W), xb, xf, l,
                        w_proj, wa, wb, wo, g_mix, b_mix)
        xf, xb = _conv_ffn(xb, xf, l, wu, conv_w, cb, wd, g_ffn, b_ffn, s)
    return xf.reshape(b, s, d)
```
